```python
import math
import jax, jax.numpy as jnp
from jax import lax
import numpy as np

D_MODEL = 1024
BATCH = 8
SEQ = 2048
DEPTH = 2
DEC_BATCH = 32
DEC_SEQ = 4
PAST_LEN = 16384
PAGE_SIZE = 128

HEAD_DIM = 64
H_DIFF = 4
H_FOX = 8
DIFF_VDIM = 2 * HEAD_DIM
D_DIFF = H_DIFF * DIFF_VDIM
D_FOX = H_FOX * HEAD_DIM
D_MIX = D_DIFF + D_FOX
D_FF = 2816
Q_BLOCK = 128
FORGET_BIAS_INIT = 4.0
EPS = 1e-6
NEG = -1e30
ATTN_SCALE = HEAD_DIM ** -0.5
N_QD = H_DIFF * 2 * HEAD_DIM
N_KD = H_DIFF * 2 * HEAD_DIM
N_VD = D_DIFF
N_QF = D_FOX
N_KF = D_FOX
N_VF = D_FOX
N_F = H_FOX
N_IN = N_QD + N_KD + N_VD + N_QF + N_KF + N_VF + N_F
_OFF = [int(o) for o in np.cumsum([0, N_QD, N_KD, N_VD, N_QF, N_KF, N_VF])]

kernel_name = "hybrid_diff_fox_macaron_decode_step"

F32 = jnp.float32


def _rms(x, g):
    xf = x.astype(F32)
    y = xf * lax.rsqrt(jnp.mean(xf * xf, axis=-1, keepdims=True) + EPS) * g.astype(F32)
    return y.astype(x.dtype)


def _swiglu(u, wg, wu, wd):
    return (jax.nn.silu(u @ wg) * (u @ wu)) @ wd


def _alibi_slopes():
    return 2.0 ** (-8.0 * jnp.arange(1, H_DIFF + 1, dtype=F32) / H_DIFF)


def _project(u, w_in, b_f, g_qd, g_kd, g_qf, g_kf):
    b, t = u.shape[:2]
    p = u @ w_in
    qd = _rms(p[..., _OFF[0]:_OFF[1]].reshape(b, t, H_DIFF, 2, HEAD_DIM), g_qd)
    kd = _rms(p[..., _OFF[1]:_OFF[2]].reshape(b, t, H_DIFF, 2, HEAD_DIM), g_kd)
    vd = p[..., _OFF[2]:_OFF[3]].reshape(b, t, H_DIFF, DIFF_VDIM)
    qf = _rms(p[..., _OFF[3]:_OFF[4]].reshape(b, t, H_FOX, HEAD_DIM), g_qf)
    kf = _rms(p[..., _OFF[4]:_OFF[5]].reshape(b, t, H_FOX, HEAD_DIM), g_kf)
    vf = p[..., _OFF[5]:_OFF[6]].reshape(b, t, H_FOX, HEAD_DIM)
    logf = jax.nn.log_sigmoid((p[..., _OFF[6]:] + b_f).astype(F32))
    return qd, kd, vd, qf, kf, vf, logf


def _diff_scores(qd, kd, dist):
    s = jnp.einsum('bthmd,bkhmd->bhmtk', qd.astype(F32), kd.astype(F32)) * ATTN_SCALE
    return s - _alibi_slopes()[None, :, None, None, None] * dist


def _fox_scores(qf, kf, bias):
    s = jnp.einsum('bthd,bkhd->bhtk', qf.astype(F32), kf.astype(F32)) * ATTN_SCALE + bias
    return s[:, :, None]


def _init_carry(b, h, m, t, dv):
    return (jnp.full((b, h, m, t), NEG, F32), jnp.zeros((b, h, m, t), F32),
            jnp.zeros((b, h, m, t, dv), F32))


def _flash_step(carry, s, v):
    m, l, acc = carry
    m_new = jnp.maximum(m, s.max(-1))
    alpha = jnp.exp(m - m_new)
    p = jnp.exp(s - m_new[..., None])
    l = l * alpha + p.sum(-1)
    acc = acc * alpha[..., None] + jnp.einsum('bhmtk,bkhd->bhmtd', p, v.astype(F32))
    return (m_new, l, acc)


def _merge_heads(cd, cf, lam, lam_init, g_subln):
    od = cd[2] / cd[1][..., None]
    od = od[:, :, 0] - lam * od[:, :, 1]
    od = _rms(od, g_subln) * (1.0 - lam_init)
    of = (cf[2] / cf[1][..., None])[:, :, 0]
    b, _, t, _ = od.shape
    return jnp.concatenate([od.transpose(0, 2, 1, 3).reshape(b, t, D_DIFF),
                            of.transpose(0, 2, 1, 3).reshape(b, t, D_FOX)], axis=-1)


def _prompt_mixer(qd, kd, vd, qf, kf, vf, logf, lam, lam_init, g_subln):
    b, s_len = qd.shape[:2]
    nb = s_len // Q_BLOCK
    c = lax.cumsum(logf, axis=1)
    ck = c.transpose(0, 2, 1)[:, :, None, :]
    k_pos = jnp.arange(s_len)

    def blk(xs):
        i, qd_b, qf_b, cq_b = xs
        q_pos = i * Q_BLOCK + jnp.arange(Q_BLOCK)
        dist = (q_pos[:, None] - k_pos[None, :]).astype(F32)
        causal = dist >= 0
        sd = jnp.where(causal, _diff_scores(qd_b, kd, dist), NEG)
        bias = cq_b.transpose(0, 2, 1)[..., None] - ck
        sf = jnp.where(causal, _fox_scores(qf_b, kf, bias), NEG)
        cd = _flash_step(_init_carry(b, H_DIFF, 2, Q_BLOCK, DIFF_VDIM), sd, vd)
        cf = _flash_step(_init_carry(b, H_FOX, 1, Q_BLOCK, HEAD_DIM), sf, vf)
        return _merge_heads(cd, cf, lam, lam_init, g_subln)

    def to_blocks(a):
        return jnp.moveaxis(a.reshape(b, nb, Q_BLOCK, *a.shape[2:]), 1, 0)

    out = lax.map(blk, (jnp.arange(nb), to_blocks(qd), to_blocks(qf), to_blocks(c)))
    return jnp.moveaxis(out, 0, 1).reshape(b, s_len, D_MIX)


def _sample_mixer(qd, kd, vd, qf, kf, vf, logf, layer, cache_k_diff, cache_v_diff, cache_k_fox,
                  cache_v_fox, cache_logf_fox, page_table, lam, lam_init, g_subln):
    b, t = qd.shape[:2]
    n_pages = page_table.shape[1]
    q_pos = PAST_LEN + jnp.arange(t)
    cn = lax.cumsum(logf, axis=1)
    cq = cn.transpose(0, 2, 1)[..., None]
    init = (_init_carry(b, H_DIFF, 2, t, DIFF_VDIM), _init_carry(b, H_FOX, 1, t, HEAD_DIM),
            jnp.zeros((b, H_FOX), F32))

    def page_step(carry, xs):
        cd, cf, s_after = carry
        j, pages = xs
        kd_p = cache_k_diff[layer, pages]
        vd_p = cache_v_diff[layer, pages]
        kf_p = cache_k_fox[layer, pages]
        vf_p = cache_v_fox[layer, pages]
        lf_p = cache_logf_fox[layer, pages].astype(F32)
        k_pos = j * PAGE_SIZE + jnp.arange(PAGE_SIZE)
        dist = (q_pos[:, None] - k_pos[None, :]).astype(F32)
        cd = _flash_step(cd, _diff_scores(qd, kd_p, dist), vd_p)
        r = s_after[:, None, :] + lax.cumsum(lf_p, axis=1, reverse=True) - lf_p
        cf = _flash_step(cf, _fox_scores(qf, kf_p, cq + r.transpose(0, 2, 1)[:, :, None, :]), vf_p)
        return (cd, cf, s_after + lf_p.sum(1)), None

    (cd, cf, _), _ = lax.scan(page_step, init, (jnp.arange(n_pages), page_table.T), reverse=True)
    dist = (q_pos[:, None] - q_pos[None, :]).astype(F32)
    causal = dist >= 0
    cd = _flash_step(cd, jnp.where(causal, _diff_scores(qd, kd, dist), NEG), vd)
    bias = cq - cn.transpose(0, 2, 1)[:, :, None, :]
    cf = _flash_step(cf, jnp.where(causal, _fox_scores(qf, kf, bias), NEG), vf)
    return _merge_heads(cd, cf, lam, lam_init, g_subln)


def _trunk_layer(x, mixer, g1, wg1, wu1, wd1, ga, w_in, b_f, g_qd, g_kd, g_qf, g_kf, w_out,
                 g2, wg2, wu2, wd2):
    h = x + 0.5 * _swiglu(_rms(x, g1), wg1, wu1, wd1)
    proj = _project(_rms(h, ga), w_in, b_f, g_qd, g_kd, g_qf, g_kf)
    h = h + mixer(*proj).astype(h.dtype) @ w_out
    y = h + 0.5 * _swiglu(_rms(h, g2), wg2, wu2, wd2)
    return y, proj


def setup_inputs(seed: int = 0) -> dict:
    key = jax.random.key(seed)
    ks = jax.random.split(key, 32)
    n_pages = PAST_LEN // PAGE_SIZE
    n_used = DEC_BATCH * n_pages
    n_pool = n_used + max(1, n_used // 4)
    nrm = jax.random.normal

    def gain(k, shape):
        return 1.0 + 0.02 * nrm(k, shape, F32)

    perm = jax.random.permutation(ks[7], n_pool)
    page_table = perm[:n_used].reshape(DEC_BATCH, n_pages).astype(jnp.int32)
    return {
        "x_prompt": nrm(ks[0], (BATCH, SEQ, D_MODEL), F32),
        "x_sample": nrm(ks[1], (DEC_BATCH, DEC_SEQ, D_MODEL), F32),
        "cache_k_diff": nrm(ks[2], (DEPTH, n_pool, PAGE_SIZE, H_DIFF, 2, HEAD_DIM), F32),
        "cache_v_diff": nrm(ks[3], (DEPTH, n_pool, PAGE_SIZE, H_DIFF, DIFF_VDIM), F32),
        "cache_k_fox": nrm(ks[4], (DEPTH, n_pool, PAGE_SIZE, H_FOX, HEAD_DIM), F32),
        "cache_v_fox": nrm(ks[5], (DEPTH, n_pool, PAGE_SIZE, H_FOX, HEAD_DIM), F32),
        "cache_logf_fox": jax.nn.log_sigmoid(FORGET_BIAS_INIT + nrm(ks[6], (DEPTH, n_pool, PAGE_SIZE, H_FOX), F32)),
        "page_table": page_table,
        "g_ffn1": gain(ks[8], (DEPTH, D_MODEL)),
        "w1_gate": nrm(ks[9], (DEPTH, D_MODEL, D_FF), F32) * D_MODEL ** -0.5,
        "w1_up": nrm(ks[10], (DEPTH, D_MODEL, D_FF), F32) * D_MODEL ** -0.5,
        "w1_down": nrm(ks[11], (DEPTH, D_FF, D_MODEL), F32) * D_FF ** -0.5,
        "g_attn": gain(ks[12], (DEPTH, D_MODEL)),
        "w_in": nrm(ks[13], (DEPTH, D_MODEL, N_IN), F32) * D_MODEL ** -0.5,
        "b_f": FORGET_BIAS_INIT + 0.1 * nrm(ks[14], (DEPTH, H_FOX), F32),
        "g_qd": gain(ks[15], (DEPTH, HEAD_DIM)),
        "g_kd": gain(ks[16], (DEPTH, HEAD_DIM)),
        "g_qf": gain(ks[17], (DEPTH, HEAD_DIM)),
        "g_kf": gain(ks[18], (DEPTH, HEAD_DIM)),
        "lam_q1": 0.1 * nrm(ks[19], (DEPTH, HEAD_DIM), F32),
        "lam_k1": 0.1 * nrm(ks[20], (DEPTH, HEAD_DIM), F32),
        "lam_q2": 0.1 * nrm(ks[21], (DEPTH, HEAD_DIM), F32),
        "lam_k2": 0.1 * nrm(ks[22], (DEPTH, HEAD_DIM), F32),
        "g_subln": gain(ks[23], (DEPTH, DIFF_VDIM)),
        "w_out": nrm(ks[24], (DEPTH, D_MIX, D_MODEL), F32) * D_MIX ** -0.5,
        "g_ffn2": gain(ks[25], (DEPTH, D_MODEL)),
        "w2_gate": nrm(ks[26], (DEPTH, D_MODEL, D_FF), F32) * D_MODEL ** -0.5,
        "w2_up": nrm(ks[27], (DEPTH, D_MODEL, D_FF), F32) * D_MODEL ** -0.5,
        "w2_down": nrm(ks[28], (DEPTH, D_FF, D_MODEL), F32) * D_FF ** -0.5,
    }


def reference(x_prompt, x_sample, cache_k_diff, cache_v_diff, cache_k_fox, cache_v_fox,
              cache_logf_fox, page_table, g_ffn1, w1_gate, w1_up, w1_down, g_attn, w_in, b_f,
              g_qd, g_kd, g_qf, g_kf, lam_q1, lam_k1, lam_q2, lam_k2, g_subln, w_out, g_ffn2,
              w2_gate, w2_up, w2_down):
    xp, xs = x_prompt, x_sample
    kd_p, vd_p, kf_p, vf_p, lf_p = [], [], [], [], []
    kd_s, vd_s, kf_s, vf_s, lf_s = [], [], [], [], []
    for l in range(DEPTH):
        lam_init = 0.8 - 0.6 * math.exp(-0.3 * l)
        lam = (jnp.exp(jnp.sum(lam_q1[l].astype(F32) * lam_k1[l].astype(F32)))
               - jnp.exp(jnp.sum(lam_q2[l].astype(F32) * lam_k2[l].astype(F32))) + lam_init)
        w_l = (g_ffn1[l], w1_gate[l], w1_up[l], w1_down[l], g_attn[l], w_in[l], b_f[l], g_qd[l],
               g_kd[l], g_qf[l], g_kf[l], w_out[l], g_ffn2[l], w2_gate[l], w2_up[l], w2_down[l])
        g_sub = g_subln[l]

        def prompt_mix(qd, kd, vd, qf, kf, vf, logf, lam=lam, lam_init=lam_init, g_sub=g_sub):
            return _prompt_mixer(qd, kd, vd, qf, kf, vf, logf, lam, lam_init, g_sub)

        def sample_mix(qd, kd, vd, qf, kf, vf, logf, lam=lam, lam_init=lam_init, g_sub=g_sub, l=l):
            return _sample_mixer(qd, kd, vd, qf, kf, vf, logf, l, cache_k_diff, cache_v_diff,
                                 cache_k_fox, cache_v_fox, cache_logf_fox, page_table, lam,
                                 lam_init, g_sub)

        xp, pp = _trunk_layer(xp, prompt_mix, *w_l)
        xs, ps = _trunk_layer(xs, sample_mix, *w_l)
        kd_p.append(pp[1]); vd_p.append(pp[2]); kf_p.append(pp[4]); vf_p.append(pp[5]); lf_p.append(pp[6])
        kd_s.append(ps[1]); vd_s.append(ps[2]); kf_s.append(ps[4]); vf_s.append(ps[5]); lf_s.append(ps[6])
    return (xp, xs, jnp.stack(kd_p), jnp.stack(vd_p), jnp.stack(kf_p), jnp.stack(vf_p), jnp.stack(lf_p),
            jnp.stack(kd_s), jnp.stack(vd_s), jnp.stack(kf_s), jnp.stack(vf_s), jnp.stack(lf_s))
```

```python
import functools
import math

import jax
import jax.numpy as jnp
from jax import lax
from jax.experimental import pallas as pl
from jax.experimental.pallas import tpu as pltpu

F32 = jnp.float32
BF16 = jnp.bfloat16

HEAD_DIM = 64
H_DIFF = 4
H_FOX = 8
DIFF_VDIM = 2 * HEAD_DIM
D_DIFF = H_DIFF * DIFF_VDIM
D_FOX = H_FOX * HEAD_DIM
D_MIX = D_DIFF + D_FOX
PAGE = 128
EPS = 1e-6
NEG = -1e30
ATTN_SCALE = HEAD_DIM ** -0.5
N_MAIN = 6 * 512

LANES = 128
SUBLANES = 8
VMEM_LIMIT = 56 * 1024 * 1024

TM = 512
TF = 256
TQ = 512
PP = 8
TPAD = 8


def _cparams(sem):
    return pltpu.CompilerParams(dimension_semantics=sem, vmem_limit_bytes=VMEM_LIMIT)


def _const_spec(shape):
    nd = len(shape)
    return pl.BlockSpec(shape, lambda *_: (0,) * nd, pipeline_mode=pl.Buffered(1))


def _rms(x, g):
    return x * lax.rsqrt(jnp.mean(x * x, axis=-1, keepdims=True) + EPS) * g


def _split3(x):
    p0 = x.astype(BF16)
    r1 = x - p0.astype(F32)
    p1 = r1.astype(BF16)
    p2 = (r1 - p1.astype(F32)).astype(BF16)
    return p0, p1, p2


def _dot(a, b):
    return jnp.dot(a, b, preferred_element_type=F32)


def _dot_nt(a, b):
    return lax.dot_general(a, b, (((1,), (1,)), ((), ())), preferred_element_type=F32)


def _exact_dot(t, x):
    p0, p1, p2 = _split3(x)
    return _dot(t, p0) + _dot(t, p1) + _dot(t, p2)


def _exact_dot_r(x, t):
    p0, p1, p2 = _split3(x)
    return _dot(p0, t) + _dot(p1, t) + _dot(p2, t)


def _ffn_body(*refs, n_chunks, fuse_out):
    if fuse_out:
        x_ref, mix_ref, wo_ref, g_ref, wg_ref, wu_ref, wd_ref, o_ref, acc_ref, u_ref = refs
        x = x_ref[...] + _dot(mix_ref[...].astype(BF16), wo_ref[...])
    else:
        x_ref, g_ref, wg_ref, wu_ref, wd_ref, o_ref, acc_ref, u_ref = refs
        x = x_ref[...]
    o_ref[...] = x
    u_ref[...] = _rms(x, g_ref[...]).astype(BF16)
    acc_ref[...] = jnp.zeros_like(acc_ref)

    def chunk(c, carry):
        u = u_ref[...]
        gate = _dot(u, wg_ref[c])
        up = _dot(u, wu_ref[c])
        a = (gate * jax.nn.sigmoid(gate) * up).astype(BF16)
        acc_ref[...] += _dot(a, wd_ref[c])
        return carry

    lax.fori_loop(0, n_chunks, chunk, 0)
    o_ref[...] = o_ref[...] + 0.5 * acc_ref[...]


def _ffn(x, g, wg_c, wu_c, wd_c, mix=None, w_out=None):
    n, d = x.shape
    n_chunks, _, tf = wg_c.shape
    tm = min(TM, n)
    fuse = mix is not None
    tok = pl.BlockSpec((tm, d), lambda i: (i, 0))
    in_specs = [tok]
    args = [x]
    if fuse:
        in_specs += [pl.BlockSpec((tm, mix.shape[1]), lambda i: (i, 0)), _const_spec(w_out.shape)]
        args += [mix, w_out]
    in_specs += [_const_spec((1, d)), _const_spec(wg_c.shape), _const_spec(wu_c.shape), _const_spec(wd_c.shape)]
    args += [g.reshape(1, d), wg_c, wu_c, wd_c]
    return pl.pallas_call(
        functools.partial(_ffn_body, n_chunks=n_chunks, fuse_out=fuse),
        grid=(n // tm,),
        in_specs=in_specs,
        out_specs=tok,
        out_shape=jax.ShapeDtypeStruct((n, d), F32),
        scratch_shapes=[pltpu.VMEM((tm, d), F32), pltpu.VMEM((tm, d), BF16)],
        compiler_params=_cparams(("arbitrary",)),
        name="ffn_out" if fuse else "ffn",
    )(*args)


def _proj_body(h_ref, ga_ref, win_ref, wf_ref, bf_ref, gqd_ref, gkd_ref, gqf_ref, gkf_ref,
               qd_o, qf_o, kd_o, vd_o, kf_o, vf_o, lf_o, kdb_o, vdb_o, kfb_o, vfb_o, c_o, ct_o,
               carry_ref, *, tm, seg):
    i = pl.program_id(0)
    u = _rms(h_ref[...], ga_ref[...]).astype(BF16)

    r = lax.broadcasted_iota(jnp.int32, (2 * LANES, 2 * LANES), 0)
    c = lax.broadcasted_iota(jnp.int32, (2 * LANES, 2 * LANES), 1)
    group = (r // HEAD_DIM == c // HEAD_DIM).astype(BF16)

    def head_norm(x, g):
        x2 = (x * x).astype(BF16)
        ss = jnp.concatenate([_dot(x2[:, :2 * LANES], group), _dot(x2[:, 2 * LANES:], group)], axis=1)
        return x * lax.rsqrt(ss * (1.0 / HEAD_DIM) + EPS) * g

    def seg_cols(k):
        return _dot(u, win_ref[:, k * 512:(k + 1) * 512])

    qd_o[...] = (head_norm(seg_cols(0), gqd_ref[...]) * ATTN_SCALE).astype(qd_o.dtype)
    kd = head_norm(seg_cols(1), gkd_ref[...])
    kd_o[...] = kd
    kdb_o[...] = kd.astype(BF16)
    vd = seg_cols(2)
    vd_o[...] = vd
    vdb_o[...] = vd.astype(BF16)
    qf_o[...] = (head_norm(seg_cols(3), gqf_ref[...]) * ATTN_SCALE).astype(qf_o.dtype)
    kf = head_norm(seg_cols(4), gkf_ref[...])
    kf_o[...] = kf
    kfb_o[...] = kf.astype(BF16)
    vf = seg_cols(5)
    vf_o[...] = vf
    vfb_o[...] = vf.astype(BF16)

    lane = lax.broadcasted_iota(jnp.int32, (tm, LANES), 1)
    logf = jnp.where(lane < H_FOX, jax.nn.log_sigmoid(_dot(u, wf_ref[...]) + bf_ref[...]), 0.0)
    lf_o[...] = logf[:, :H_FOX]

    ri = lax.broadcasted_iota(jnp.int32, (tm, tm), 0)
    ci = lax.broadcasted_iota(jnp.int32, (tm, tm), 1)
    if seg >= tm:
        tri = (ci <= ri).astype(BF16)
        tiles_per_seg = seg // tm

        @pl.when(i % tiles_per_seg == 0)
        def _():
            carry_ref[...] = jnp.zeros_like(carry_ref)

        csum = _exact_dot(tri, logf) + carry_ref[0:1, :]
        carry_ref[...] = jnp.broadcast_to(csum[tm - 1:tm, :], carry_ref.shape)
    else:
        tri = ((ci <= ri) & (ri // seg == ci // seg)).astype(BF16)
        csum = _exact_dot(tri, logf)
    c_o[...] = csum
    ct_o[...] = csum.T[:H_FOX, :]


def _proj(h, ga, win_main, wf_pad, bf_pad, gqd, gkd, gqf, gkf, *, seg, q_dtype):
    n, d = h.shape
    tm = min(TM, n)
    tok512 = pl.BlockSpec((tm, 512), lambda i: (i, 0))
    out_shape = (
        jax.ShapeDtypeStruct((n, 512), q_dtype), jax.ShapeDtypeStruct((n, 512), q_dtype),
        jax.ShapeDtypeStruct((n, 512), F32), jax.ShapeDtypeStruct((n, 512), F32),
        jax.ShapeDtypeStruct((n, 512), F32), jax.ShapeDtypeStruct((n, 512), F32),
        jax.ShapeDtypeStruct((n, H_FOX), F32),
        jax.ShapeDtypeStruct((n, 512), BF16), jax.ShapeDtypeStruct((n, 512), BF16),
        jax.ShapeDtypeStruct((n, 512), BF16), jax.ShapeDtypeStruct((n, 512), BF16),
        jax.ShapeDtypeStruct((n, LANES), F32), jax.ShapeDtypeStruct((H_FOX, n), F32),
    )
    out_specs = (
        tok512, tok512, tok512, tok512, tok512, tok512,
        pl.BlockSpec((tm, H_FOX), lambda i: (i, 0)),
        tok512, tok512, tok512, tok512,
        pl.BlockSpec((tm, LANES), lambda i: (i, 0)), pl.BlockSpec((H_FOX, tm), lambda i: (0, i)),
    )
    return pl.pallas_call(
        functools.partial(_proj_body, tm=tm, seg=seg),
        grid=(n // tm,),
        in_specs=[pl.BlockSpec((tm, d), lambda i: (i, 0)), _const_spec((1, d)),
                  _const_spec(win_main.shape), _const_spec(wf_pad.shape), _const_spec((1, LANES)),
                  _const_spec((1, 512)), _const_spec((1, 512)), _const_spec((1, 512)), _const_spec((1, 512))],
        out_specs=out_specs,
        out_shape=out_shape,
        scratch_shapes=[pltpu.VMEM((SUBLANES, LANES), F32)],
        compiler_params=_cparams(("arbitrary",)),
        name="proj",
    )(h, ga.reshape(1, d), win_main, wf_pad, bf_pad, gqd, gkd, gqf, gkf)


def _lambda(lq1, lk1, lq2, lk2, lam_init):
    a = jnp.sum(lq1 * lk1, axis=-1, keepdims=True)
    b = jnp.sum(lq2 * lk2, axis=-1, keepdims=True)
    return jnp.exp(a) - jnp.exp(b) + lam_init


def _alibi_slope(h):
    return 2.0 ** (-8.0 * (h + 1) / H_DIFF)


def _prompt_attn_body(qd_ref, qf_ref, kd_ref, vd_ref, kf_ref, vf_ref, c_ref, ct_ref,
                      lq1_ref, lk1_ref, lq2_ref, lk2_ref, gs_ref, o_ref,
                      md_ref, ld_ref, accd_ref, mf_ref, lf_ref, accf_ref, *, tq, lam_init):
    i = pl.program_id(1)
    j = pl.program_id(2)

    @pl.when(j == 0)
    def _():
        md_ref[...] = jnp.full_like(md_ref, NEG)
        mf_ref[...] = jnp.full_like(mf_ref, NEG)
        ld_ref[...] = jnp.zeros_like(ld_ref)
        lf_ref[...] = jnp.zeros_like(lf_ref)
        accd_ref[...] = jnp.zeros_like(accd_ref)
        accf_ref[...] = jnp.zeros_like(accf_ref)

    lane = lax.broadcasted_iota(jnp.int32, (tq, LANES), 1)
    lo = lane < HEAD_DIM

    @pl.when(j <= i)
    def _():
        row = i * tq + lax.broadcasted_iota(jnp.int32, (tq, tq), 0)
        col = j * tq + lax.broadcasted_iota(jnp.int32, (tq, tq), 1)
        dist = (row - col).astype(F32)
        causal = dist >= 0.0

        def online(m_ref, l_ref, u, s):
            m_prev = m_ref[u]
            m_new = jnp.maximum(m_prev, jnp.max(s, axis=-1, keepdims=True))
            alpha = jnp.exp(m_prev - m_new)
            p = jnp.exp(s - m_new)
            l_ref[u] = alpha * l_ref[u] + jnp.sum(p, axis=-1, keepdims=True)
            m_ref[u] = m_new
            return alpha, p.astype(BF16)

        for h in range(H_DIFF):
            sl = slice(h * LANES, (h + 1) * LANES)
            q = qd_ref[:, sl]
            k = kd_ref[:, sl]
            v = vd_ref[:, sl]
            bias = jnp.where(causal, -(_alibi_slope(h) * dist), NEG)
            for m in range(2):
                qm = jnp.where(lo if m == 0 else ~lo, q, jnp.zeros_like(q))
                s = _dot_nt(qm, k) + bias
                alpha, p = online(md_ref, ld_ref, 2 * h + m, s)
                accd_ref[2 * h + m] = alpha * accd_ref[2 * h + m] + _dot(p, v)

        for g in range(H_FOX // 2):
            sl = slice(g * LANES, (g + 1) * LANES)
            q = qf_ref[:, sl]
            k = kf_ref[:, sl]
            v = vf_ref[:, sl]
            alphas = []
            pv = None
            for e in range(2):
                hh = 2 * g + e
                sel = lo if e == 0 else ~lo
                qe = jnp.where(sel, q, jnp.zeros_like(q))
                ve = jnp.where(sel, v, jnp.zeros_like(v))
                bias = jnp.where(causal, c_ref[:, hh:hh + 1] - ct_ref[hh:hh + 1, :], NEG)
                s = _dot_nt(qe, k) + bias
                alpha, p = online(mf_ref, lf_ref, hh, s)
                alphas.append(alpha)
                d = _dot(p, ve)
                pv = d if pv is None else pv + d
            accf_ref[g] = jnp.where(lo, alphas[0], alphas[1]) * accf_ref[g] + pv

    @pl.when(j == i)
    def _():
        lam = _lambda(lq1_ref[...], lk1_ref[...], lq2_ref[...], lk2_ref[...], lam_init)
        for h in range(H_DIFF):
            od0 = accd_ref[2 * h] / ld_ref[2 * h]
            od1 = accd_ref[2 * h + 1] / ld_ref[2 * h + 1]
            o = _rms(od0 - lam * od1, gs_ref[...]) * (1.0 - lam_init)
            o_ref[:, h * LANES:(h + 1) * LANES] = o.astype(o_ref.dtype)
        for g in range(H_FOX // 2):
            lsum = jnp.where(lo, lf_ref[2 * g], lf_ref[2 * g + 1])
            o_ref[:, D_DIFF + g * LANES:D_DIFF + (g + 1) * LANES] = (accf_ref[g] / lsum).astype(o_ref.dtype)


def _prompt_attn(qd, qf, kdb, vdb, kfb, vfb, c, ct, lam_vecs, g_sub, *, batch, seq, lam_init):
    tq = min(TQ, seq)
    nq = seq // tq
    q_spec = pl.BlockSpec((tq, 512), lambda b, i, j: (b * nq + i, 0))
    k_spec = pl.BlockSpec((tq, 512), lambda b, i, j: (b * nq + jnp.minimum(i, j), 0))
    vec64 = _const_spec((1, HEAD_DIM))
    return pl.pallas_call(
        functools.partial(_prompt_attn_body, tq=tq, lam_init=lam_init),
        grid=(batch, nq, nq),
        in_specs=[q_spec, q_spec, k_spec, k_spec, k_spec, k_spec,
                  pl.BlockSpec((tq, LANES), lambda b, i, j: (b * nq + i, 0)),
                  pl.BlockSpec((H_FOX, tq), lambda b, i, j: (0, b * nq + jnp.minimum(i, j))),
                  vec64, vec64, vec64, vec64, _const_spec((1, DIFF_VDIM))],
        out_specs=pl.BlockSpec((tq, D_MIX), lambda b, i, j: (b * nq + i, 0)),
        out_shape=jax.ShapeDtypeStruct((batch * seq, D_MIX), BF16),
        scratch_shapes=[pltpu.VMEM((2 * H_DIFF, tq, 1), F32), pltpu.VMEM((2 * H_DIFF, tq, 1), F32),
                        pltpu.VMEM((2 * H_DIFF, tq, DIFF_VDIM), F32),
                        pltpu.VMEM((H_FOX, tq, 1), F32), pltpu.VMEM((H_FOX, tq, 1), F32),
                        pltpu.VMEM((H_FOX // 2, tq, LANES), F32)],
        compiler_params=_cparams(("arbitrary", "arbitrary", "arbitrary")),
        name="prompt_attn",
    )(qd, qf, kdb, vdb, kfb, vfb, c, ct, *lam_vecs, g_sub)


def _sample_attn_body(pt_ref, qd_ref, qf_ref, kdn_ref, vdn_ref, kfn_ref, vfn_ref, cn_ref,
                      lq1_ref, lk1_ref, lq2_ref, lk2_ref, gs_ref, *rest, n_pages, lam_init):
    kd_refs = rest[0 * PP:1 * PP]
    vd_refs = rest[1 * PP:2 * PP]
    kf_refs = rest[2 * PP:3 * PP]
    vf_refs = rest[3 * PP:4 * PP]
    lf_refs = rest[4 * PP:5 * PP]
    o_ref, qbd_ref, qbf_ref, md_ref, ld_ref, accd_ref, mf_ref, lf_ref, accf_ref, sa_ref, cqb_ref = rest[5 * PP:]
    g = pl.program_id(1)
    n_steps = n_pages // PP
    rows = 2 * H_DIFF * TPAD

    row_i = lax.broadcasted_iota(jnp.int32, (rows, LANES), 0)
    lane_i = lax.broadcasted_iota(jnp.int32, (rows, LANES), 1)
    t_of_row = row_i % TPAD

    @pl.when(g == 0)
    def _():
        r512 = lax.broadcasted_iota(jnp.int32, (rows, 512), 0)
        l512 = lax.broadcasted_iota(jnp.int32, (rows, 512), 1)
        unit = r512 // TPAD == l512 // HEAD_DIM
        qd8 = jnp.concatenate([qd_ref[0]] * (rows // TPAD), axis=0)
        qf8 = jnp.concatenate([qf_ref[0]] * (rows // TPAD), axis=0)
        qbd_ref[...] = jnp.where(unit, qd8, 0.0).astype(BF16)
        qbf_ref[...] = jnp.where(unit, qf8, 0.0).astype(BF16)
        md_ref[...] = jnp.full_like(md_ref, NEG)
        mf_ref[...] = jnp.full_like(mf_ref, NEG)
        ld_ref[...] = jnp.zeros_like(ld_ref)
        lf_ref[...] = jnp.zeros_like(lf_ref)
        accd_ref[...] = jnp.zeros_like(accd_ref)
        accf_ref[...] = jnp.zeros_like(accf_ref)
        sa_ref[...] = jnp.zeros_like(sa_ref)
        cn = cn_ref[0]
        cqb_ref[...] = jnp.concatenate(
            [jnp.broadcast_to(cn[:, h:h + 1], (TPAD, LANES)) for h in range(H_FOX)], axis=0)

    slope = jnp.zeros((rows, LANES), F32)
    for h in range(H_DIFF):
        slope = jnp.where(row_i // (2 * TPAD) == h, _alibi_slope(h), slope)

    def online(m_ref, l_ref, s):
        m_prev = m_ref[...]
        m_new = jnp.maximum(m_prev, jnp.max(s, axis=-1, keepdims=True))
        alpha = jnp.exp(m_prev - m_new)
        p = jnp.exp(s - m_new)
        l_ref[...] = alpha * l_ref[...] + jnp.sum(p, axis=-1, keepdims=True)
        m_ref[...] = m_new
        return alpha, p.astype(BF16)

    p0 = n_pages - (g + 1) * PP
    ki = lax.broadcasted_iota(jnp.int32, (PAGE, PAGE), 0)
    kj = lax.broadcasted_iota(jnp.int32, (PAGE, PAGE), 1)
    later = (ki > kj).astype(BF16)
    qbd = qbd_ref[...]
    qbf = qbf_ref[...]
    past_len = n_pages * PAGE

    sd = [None] * PP
    sf = [None] * PP
    s_after = sa_ref[...]
    for r in reversed(range(PP)):
        lft = lf_refs[r][...]
        rt = _exact_dot_r(lft, later) + s_after
        s_after = s_after + jnp.sum(lft, axis=-1, keepdims=True)
        rt_rows = jnp.concatenate(
            [jnp.broadcast_to(rt[h:h + 1, :], (TPAD, LANES)) for h in range(H_FOX)], axis=0)
        sf[r] = _dot(qbf, kf_refs[r][...].astype(BF16)) + (cqb_ref[...] + rt_rows)
        off = past_len - (p0 + r) * PAGE
        dist = (t_of_row - lane_i + off).astype(F32)
        sd[r] = _dot(qbd, kd_refs[r][...].astype(BF16)) - slope * dist
    sa_ref[...] = s_after

    alpha, p = online(md_ref, ld_ref, jnp.concatenate(sd, axis=1))
    pv_heads = []
    for h in range(H_DIFF):
        acc = None
        for r in range(PP):
            v = vd_refs[r][pl.ds(h, PAGE, stride=H_DIFF), :].astype(BF16)
            d = _dot(p[h * 2 * TPAD:(h + 1) * 2 * TPAD, r * PAGE:(r + 1) * PAGE], v)
            acc = d if acc is None else acc + d
        pv_heads.append(acc)
    accd_ref[...] = alpha * accd_ref[...] + jnp.concatenate(pv_heads, axis=0)

    alpha, p = online(mf_ref, lf_ref, jnp.concatenate(sf, axis=1))
    acc = None
    for r in range(PP):
        d = _dot_nt(p[:, r * PAGE:(r + 1) * PAGE], vf_refs[r][...].astype(BF16))
        acc = d if acc is None else acc + d
    accf_ref[...] = alpha * accf_ref[...] + acc

    @pl.when(g == n_steps - 1)
    def _():
        rr = lax.broadcasted_iota(jnp.int32, (rows, TPAD), 0) % TPAD
        ss = lax.broadcasted_iota(jnp.int32, (rows, TPAD), 1)
        causal = ss <= rr
        dist_n = (rr - ss).astype(F32)

        s = _dot_nt(qbd, kdn_ref[0].astype(BF16)) - slope[:, :TPAD] * dist_n
        alpha, p = online(md_ref, ld_ref, jnp.where(causal, s, NEG))
        vdn = vdn_ref[0].astype(BF16)
        pv = jnp.concatenate(
            [_dot(p[h * 2 * TPAD:(h + 1) * 2 * TPAD, :], vdn[:, h * LANES:(h + 1) * LANES])
             for h in range(H_DIFF)], axis=0)
        accd = alpha * accd_ref[...] + pv

        cn = cn_ref[0]
        cnt = cn.T
        bias = jnp.concatenate(
            [cn[:, h:h + 1] - cnt[h:h + 1, :] for h in range(H_FOX)], axis=0)
        s = _dot_nt(qbf, kfn_ref[0].astype(BF16)) + bias
        alpha, p = online(mf_ref, lf_ref, jnp.where(causal, s, NEG))
        accf = alpha * accf_ref[...] + _dot(p, vfn_ref[0].astype(BF16))

        lam = _lambda(lq1_ref[...], lk1_ref[...], lq2_ref[...], lk2_ref[...], lam_init)
        od = accd / ld_ref[...]
        for h in range(H_DIFF):
            od0 = od[(2 * h) * TPAD:(2 * h + 1) * TPAD]
            od1 = od[(2 * h + 1) * TPAD:(2 * h + 2) * TPAD]
            o = _rms(od0 - lam * od1, gs_ref[...]) * (1.0 - lam_init)
            o_ref[0, :, h * LANES:(h + 1) * LANES] = o
        of = accf / lf_ref[...]
        l512 = lax.broadcasted_iota(jnp.int32, (TPAD, D_FOX), 1)
        out_f = jnp.zeros((TPAD, D_FOX), F32)
        for h in range(H_FOX):
            out_f = jnp.where(l512 // HEAD_DIM == h, of[h * TPAD:(h + 1) * TPAD], out_f)
        o_ref[0, :, D_DIFF:] = out_f


def _sample_attn(page_table, qd, qf, kdn, vdn, kfn, vfn, cn, lam_vecs, g_sub,
                 kd_pages, vd_pages, kf_pages, vf_pages, lf_pages, *, layer, lam_init):
    batch, n_pages = page_table.shape
    assert n_pages % PP == 0
    n_steps = n_pages // PP
    rows = 2 * H_DIFF * TPAD

    def tok_spec(width):
        return pl.BlockSpec((1, TPAD, width), lambda b, g, pt: (b, 0, 0))

    def page_spec(r, nrow):
        def imap(b, g, pt):
            return (layer, pt[b * n_pages + n_pages - (g + 1) * PP + r], 0, 0)
        return pl.BlockSpec((None, None, nrow, LANES), imap)

    vec64 = pl.BlockSpec((1, HEAD_DIM), lambda b, g, pt: (0, 0))
    in_specs = [tok_spec(512), tok_spec(512), tok_spec(512), tok_spec(512), tok_spec(512), tok_spec(512),
                tok_spec(LANES), vec64, vec64, vec64, vec64,
                pl.BlockSpec((1, DIFF_VDIM), lambda b, g, pt: (0, 0))]
    args = [qd, qf, kdn, vdn, kfn, vfn, cn, *lam_vecs, g_sub]
    for arr, nrow in ((kd_pages, 512), (vd_pages, 512), (kf_pages, 512), (vf_pages, 512), (lf_pages, H_FOX)):
        for r in range(PP):
            in_specs.append(page_spec(r, nrow))
            args.append(arr)
    grid_spec = pltpu.PrefetchScalarGridSpec(
        num_scalar_prefetch=1,
        grid=(batch, n_steps),
        in_specs=in_specs,
        out_specs=pl.BlockSpec((1, TPAD, D_MIX), lambda b, g, pt: (b, 0, 0)),
        scratch_shapes=[pltpu.VMEM((rows, 512), BF16), pltpu.VMEM((rows, 512), BF16),
                        pltpu.VMEM((rows, 1), F32), pltpu.VMEM((rows, 1), F32), pltpu.VMEM((rows, DIFF_VDIM), F32),
                        pltpu.VMEM((rows, 1), F32), pltpu.VMEM((rows, 1), F32), pltpu.VMEM((rows, D_FOX), F32),
                        pltpu.VMEM((H_FOX, 1), F32), pltpu.VMEM((rows, LANES), F32)],
    )
    return pl.pallas_call(
        functools.partial(_sample_attn_body, n_pages=n_pages, lam_init=lam_init),
        grid_spec=grid_spec,
        out_shape=jax.ShapeDtypeStruct((batch, TPAD, D_MIX), F32),
        compiler_params=_cparams(("arbitrary", "arbitrary")),
        name="sample_attn",
    )(page_table.reshape(-1), *args)


def _prep_ffn(wg, wu, wd):
    d, dff = wg.shape
    n_chunks = dff // TF
    wg_c = wg.astype(BF16).reshape(d, n_chunks, TF).transpose(1, 0, 2)
    wu_c = wu.astype(BF16).reshape(d, n_chunks, TF).transpose(1, 0, 2)
    wd_c = wd.astype(BF16).reshape(n_chunks, TF, d)
    return wg_c, wu_c, wd_c


def _tile_gain(g):
    return jnp.tile(g.astype(F32), 512 // HEAD_DIM).reshape(1, 512)


def _pad_t(a, t):
    b = a.shape[0] // t
    return jnp.pad(a.reshape(b, t, a.shape[1]), ((0, 0), (0, TPAD - t), (0, 0)))


def kernel(x_prompt, x_sample, cache_k_diff, cache_v_diff, cache_k_fox, cache_v_fox, cache_logf_fox, page_table, g_ffn1, w1_gate, w1_up, w1_down, g_attn, w_in, b_f, g_qd, g_kd, g_qf, g_kf, lam_q1, lam_k1, lam_q2, lam_k2, g_subln, w_out, g_ffn2, w2_gate, w2_up, w2_down):
    batch, seq, d = x_prompt.shape
    dbatch, dseq, _ = x_sample.shape
    depth = w_in.shape[0]
    n_pool = cache_k_diff.shape[1]

    kd_pages = jnp.transpose(cache_k_diff, (0, 1, 3, 4, 5, 2)).reshape(depth, n_pool, 512, PAGE)
    vd_pages = cache_v_diff.reshape(depth, n_pool, PAGE * H_DIFF, DIFF_VDIM)
    kf_pages = jnp.transpose(cache_k_fox, (0, 1, 3, 4, 2)).reshape(depth, n_pool, 512, PAGE)
    vf_pages = jnp.transpose(cache_v_fox, (0, 1, 3, 4, 2)).reshape(depth, n_pool, 512, PAGE)
    lf_pages = jnp.transpose(cache_logf_fox, (0, 1, 3, 2))

    xp = x_prompt.reshape(batch * seq, d)
    xs = x_sample.reshape(dbatch * dseq, d)
    outs_p = [[] for _ in range(5)]
    outs_s = [[] for _ in range(5)]
    for l in range(depth):
        lam_init = 0.8 - 0.6 * math.exp(-0.3 * l)
        ffn1 = _prep_ffn(w1_gate[l], w1_up[l], w1_down[l])
        ffn2 = _prep_ffn(w2_gate[l], w2_up[l], w2_down[l])
        win_main = w_in[l][:, :N_MAIN].astype(BF16)
        wf_pad = jnp.pad(w_in[l][:, N_MAIN:], ((0, 0), (0, LANES - H_FOX))).astype(BF16)
        bf_pad = jnp.pad(b_f[l].astype(F32), (0, LANES - H_FOX)).reshape(1, LANES)
        gains = (_tile_gain(g_qd[l]), _tile_gain(g_kd[l]), _tile_gain(g_qf[l]), _tile_gain(g_kf[l]))
        wo = w_out[l].astype(BF16)
        lam_vecs = tuple(v[l].astype(F32).reshape(1, HEAD_DIM) for v in (lam_q1, lam_k1, lam_q2, lam_k2))
        g_sub = g_subln[l].astype(F32).reshape(1, DIFF_VDIM)

        hp = _ffn(xp, g_ffn1[l], *ffn1)
        (qd, qf, kd, vd, kf, vf, lf, kdb, vdb, kfb, vfb, c, ct) = _proj(
            hp, g_attn[l], win_main, wf_pad, bf_pad, *gains, seg=seq, q_dtype=BF16)
        mix = _prompt_attn(qd, qf, kdb, vdb, kfb, vfb, c, ct, lam_vecs, g_sub,
                           batch=batch, seq=seq, lam_init=lam_init)
        xp = _ffn(hp, g_ffn2[l], *ffn2, mix=mix, w_out=wo)
        for lst, a in zip(outs_p, (kd, vd, kf, vf, lf)):
            lst.append(a)

        hs = _ffn(xs, g_ffn1[l], *ffn1)
        (qd, qf, kd, vd, kf, vf, lf, _, _, _, _, c, _) = _proj(
            hs, g_attn[l], win_main, wf_pad, bf_pad, *gains, seg=dseq, q_dtype=F32)
        mix = _sample_attn(page_table, _pad_t(qd, dseq), _pad_t(qf, dseq), _pad_t(kd, dseq), _pad_t(vd, dseq),
                           _pad_t(kf, dseq), _pad_t(vf, dseq), _pad_t(c, dseq), lam_vecs, g_sub,
                           kd_pages, vd_pages, kf_pages, vf_pages, lf_pages, layer=l, lam_init=lam_init)
        mix = mix[:, :dseq].reshape(dbatch * dseq, D_MIX)
        xs = _ffn(hs, g_ffn2[l], *ffn2, mix=mix, w_out=wo)
        for lst, a in zip(outs_s, (kd, vd, kf, vf, lf)):
            lst.append(a)

    def stack(lst, b, t, tail):
        return jnp.stack(lst).reshape(depth, b, t, *tail)

    tails = ((H_DIFF, 2, HEAD_DIM), (H_DIFF, DIFF_VDIM), (H_FOX, HEAD_DIM), (H_FOX, HEAD_DIM), (H_FOX,))
    return (xp.reshape(batch, seq, d), xs.reshape(dbatch, dseq, d),
            *[stack(lst, batch, seq, tail) for lst, tail in zip(outs_p, tails)],
            *[stack(lst, dbatch, dseq, tail) for lst, tail in zip(outs_s, tails)])
```

```python
import functools
import math

import jax
import jax.numpy as jnp
from jax import lax
from jax.experimental import pallas as pl
from jax.experimental.pallas import tpu as pltpu

F32 = jnp.float32
BF16 = jnp.bfloat16

HEAD_DIM = 64
H_DIFF = 4
H_FOX = 8
DIFF_VDIM = 2 * HEAD_DIM
D_DIFF = H_DIFF * DIFF_VDIM
D_FOX = H_FOX * HEAD_DIM
D_MIX = D_DIFF + D_FOX
PAGE = 128
EPS = 1e-6
NEG = -1e30
ATTN_SCALE = HEAD_DIM ** -0.5
LOG2E = math.log2(math.e)
N_MAIN = 6 * 512

LANES = 128
SUBLANES = 8
VMEM_LIMIT = 56 * 1024 * 1024

TM = 512
TF = 256
TQ = 512
PP = 8
TPAD = 8


def _cparams(sem):
    return pltpu.CompilerParams(dimension_semantics=sem, vmem_limit_bytes=VMEM_LIMIT)


def _const_spec(shape):
    nd = len(shape)
    return pl.BlockSpec(shape, lambda *_: (0,) * nd, pipeline_mode=pl.Buffered(1))


def _rms(x, g):
    return x * lax.rsqrt(jnp.mean(x * x, axis=-1, keepdims=True) + EPS) * g


def _split3(x):
    p0 = x.astype(BF16)
    r1 = x - p0.astype(F32)
    p1 = r1.astype(BF16)
    p2 = (r1 - p1.astype(F32)).astype(BF16)
    return p0, p1, p2


def _dot(a, b):
    return jnp.dot(a, b, preferred_element_type=F32)


def _dot_nt(a, b):
    return lax.dot_general(a, b, (((1,), (1,)), ((), ())), preferred_element_type=F32)


def _exact_dot(t, x):
    p0, p1, p2 = _split3(x)
    return _dot(t, p0) + _dot(t, p1) + _dot(t, p2)


def _exact_dot_r(x, t):
    p0, p1, p2 = _split3(x)
    return _dot(p0, t) + _dot(p1, t) + _dot(p2, t)


def _ffn_body(*refs, n_chunks, fuse_out):
    if fuse_out:
        x_ref, mix_ref, wo_ref, g_ref, wg_ref, wu_ref, wd_ref, o_ref, acc_ref, u_ref = refs
        x = x_ref[...] + _dot(mix_ref[...].astype(BF16), wo_ref[...])
    else:
        x_ref, g_ref, wg_ref, wu_ref, wd_ref, o_ref, acc_ref, u_ref = refs
        x = x_ref[...]
    o_ref[...] = x
    u_ref[...] = _rms(x, g_ref[...]).astype(BF16)
    acc_ref[...] = jnp.zeros_like(acc_ref)

    def chunk(c, carry):
        u = u_ref[...]
        gate = _dot(u, wg_ref[c])
        up = _dot(u, wu_ref[c])
        a = (gate * jax.nn.sigmoid(gate) * up).astype(BF16)
        acc_ref[...] += _dot(a, wd_ref[c])
        return carry

    lax.fori_loop(0, n_chunks, chunk, 0)
    o_ref[...] = o_ref[...] + 0.5 * acc_ref[...]


def _ffn(x, g, wg_c, wu_c, wd_c, mix=None, w_out=None):
    n, d = x.shape
    n_chunks, _, tf = wg_c.shape
    tm = min(TM, n)
    fuse = mix is not None
    tok = pl.BlockSpec((tm, d), lambda i: (i, 0))
    in_specs = [tok]
    args = [x]
    if fuse:
        in_specs += [pl.BlockSpec((tm, mix.shape[1]), lambda i: (i, 0)), _const_spec(w_out.shape)]
        args += [mix, w_out]
    in_specs += [_const_spec((1, d)), _const_spec(wg_c.shape), _const_spec(wu_c.shape), _const_spec(wd_c.shape)]
    args += [g.reshape(1, d), wg_c, wu_c, wd_c]
    return pl.pallas_call(
        functools.partial(_ffn_body, n_chunks=n_chunks, fuse_out=fuse),
        grid=(n // tm,),
        in_specs=in_specs,
        out_specs=tok,
        out_shape=jax.ShapeDtypeStruct((n, d), F32),
        scratch_shapes=[pltpu.VMEM((tm, d), F32), pltpu.VMEM((tm, d), BF16)],
        compiler_params=_cparams(("arbitrary",)),
        name="ffn_out" if fuse else "ffn",
    )(*args)


def _proj_body(h_ref, ga_ref, win_ref, wf_ref, bf_ref, gqd_ref, gkd_ref, gqf_ref, gkf_ref,
               qd_o, qf_o, kd_o, vd_o, kf_o, vf_o, lf_o, kdb_o, vdb_o, kfb_o, vfb_o, c_o,
               carry_ref, *, tm, seg, q_scale, native):
    i = pl.program_id(0)
    u = _rms(h_ref[...], ga_ref[...]).astype(BF16)

    def put_rows(o_ref, x):
        o_ref[...] = x.T if native else x

    r = lax.broadcasted_iota(jnp.int32, (2 * LANES, 2 * LANES), 0)
    c = lax.broadcasted_iota(jnp.int32, (2 * LANES, 2 * LANES), 1)
    group = (r // HEAD_DIM == c // HEAD_DIM).astype(BF16)

    def head_norm(x, g):
        x2 = (x * x).astype(BF16)
        ss = jnp.concatenate([_dot(x2[:, :2 * LANES], group), _dot(x2[:, 2 * LANES:], group)], axis=1)
        return x * lax.rsqrt(ss * (1.0 / HEAD_DIM) + EPS) * g

    def seg_cols(k):
        return _dot(u, win_ref[:, k * 512:(k + 1) * 512])

    qd_o[...] = (head_norm(seg_cols(0), gqd_ref[...]) * q_scale).astype(qd_o.dtype)
    kd = head_norm(seg_cols(1), gkd_ref[...])
    put_rows(kd_o, kd)
    kdb_o[...] = kd.astype(BF16)
    vd = seg_cols(2)
    if native:
        for h in range(H_DIFF):
            vd_o[pl.ds(h, tm, stride=H_DIFF), :] = vd[:, h * DIFF_VDIM:(h + 1) * DIFF_VDIM]
    else:
        vd_o[...] = vd
    vdb_o[...] = vd.astype(BF16)
    qf_o[...] = (head_norm(seg_cols(3), gqf_ref[...]) * q_scale).astype(qf_o.dtype)
    kf = head_norm(seg_cols(4), gkf_ref[...])
    put_rows(kf_o, kf)
    kfb_o[...] = kf.astype(BF16)
    vf = seg_cols(5)
    put_rows(vf_o, vf)
    vfb_o[...] = vf.astype(BF16)

    lane = lax.broadcasted_iota(jnp.int32, (tm, LANES), 1)
    logf = jnp.where(lane < H_FOX, jax.nn.log_sigmoid(_dot(u, wf_ref[...]) + bf_ref[...]), 0.0)
    lf_o[...] = logf.T[:H_FOX, :] if native else logf[:, :H_FOX]

    ri = lax.broadcasted_iota(jnp.int32, (tm, tm), 0)
    ci = lax.broadcasted_iota(jnp.int32, (tm, tm), 1)
    if seg >= tm:
        tri = (ci <= ri).astype(BF16)
        tiles_per_seg = seg // tm

        @pl.when(i % tiles_per_seg == 0)
        def _():
            carry_ref[...] = jnp.zeros_like(carry_ref)

        csum = _exact_dot(tri, logf) + carry_ref[0:1, :]
        carry_ref[...] = jnp.broadcast_to(csum[tm - 1:tm, :], carry_ref.shape)
    else:
        tri = ((ci <= ri) & (ri // seg == ci // seg)).astype(BF16)
        csum = _exact_dot(tri, logf)
    c_o[...] = csum


def _proj(h, ga, win_main, wf_pad, bf_pad, gqd, gkd, gqf, gkf, *, seg, q_dtype, q_scale, native):
    n, d = h.shape
    tm = min(TM, n)
    tok512 = pl.BlockSpec((tm, 512), lambda i: (i, 0))
    if native:
        assert seg % tm == 0
        tps = seg // tm
        nseq = n // seg
        fm_shape = jax.ShapeDtypeStruct((nseq, 512, seg), F32)
        fm_spec = pl.BlockSpec((None, 512, tm), lambda i: (i // tps, 0, i % tps))
        k_shapes = (fm_shape, jax.ShapeDtypeStruct((n * H_DIFF, DIFF_VDIM), F32), fm_shape, fm_shape,
                    jax.ShapeDtypeStruct((nseq, H_FOX, seg), F32))
        k_specs = (fm_spec, pl.BlockSpec((tm * H_DIFF, DIFF_VDIM), lambda i: (i, 0)), fm_spec, fm_spec,
                   pl.BlockSpec((None, H_FOX, tm), lambda i: (i // tps, 0, i % tps)))
    else:
        k_shapes = (jax.ShapeDtypeStruct((n, 512), F32),) * 4 + (jax.ShapeDtypeStruct((n, H_FOX), F32),)
        k_specs = (tok512,) * 4 + (pl.BlockSpec((tm, H_FOX), lambda i: (i, 0)),)
    out_shape = (
        jax.ShapeDtypeStruct((n, 512), q_dtype), jax.ShapeDtypeStruct((n, 512), q_dtype), *k_shapes,
        jax.ShapeDtypeStruct((n, 512), BF16), jax.ShapeDtypeStruct((n, 512), BF16),
        jax.ShapeDtypeStruct((n, 512), BF16), jax.ShapeDtypeStruct((n, 512), BF16),
        jax.ShapeDtypeStruct((n, LANES), F32),
    )
    out_specs = (tok512, tok512, *k_specs, tok512, tok512, tok512, tok512,
                 pl.BlockSpec((tm, LANES), lambda i: (i, 0)))
    return pl.pallas_call(
        functools.partial(_proj_body, tm=tm, seg=seg, q_scale=q_scale, native=native),
        grid=(n // tm,),
        in_specs=[pl.BlockSpec((tm, d), lambda i: (i, 0)), _const_spec((1, d)),
                  _const_spec(win_main.shape), _const_spec(wf_pad.shape), _const_spec((1, LANES)),
                  _const_spec((1, 512)), _const_spec((1, 512)), _const_spec((1, 512)), _const_spec((1, 512))],
        out_specs=out_specs,
        out_shape=out_shape,
        scratch_shapes=[pltpu.VMEM((SUBLANES, LANES), F32)],
        compiler_params=_cparams(("arbitrary",)),
        name="proj",
    )(h, ga.reshape(1, d), win_main, wf_pad, bf_pad, gqd, gkd, gqf, gkf)


def _lambda(lq1, lk1, lq2, lk2, lam_init):
    a = jnp.sum(lq1 * lk1, axis=-1, keepdims=True)
    b = jnp.sum(lq2 * lk2, axis=-1, keepdims=True)
    return jnp.exp(a) - jnp.exp(b) + lam_init


def _alibi_slope(h):
    return 2.0 ** (-8.0 * (h + 1) / H_DIFF)


X_FOX_K = 0
X_FOX_Q = 24
X_ALIBI_Q = 32
X_ALIBI_K = 35


def _prompt_attn_body(qd_ref, qf_ref, kd_ref, vd_ref, kf_ref, vf_ref, cq_ref, ck_ref,
                      lq1_ref, lk1_ref, lq2_ref, lk2_ref, gs_ref, o_ref,
                      q2_ref, m_ref, l_ref, acc_ref, *, tq, lam_init):
    i = pl.program_id(1)
    j = pl.program_id(2)
    n_chain = H_DIFF + H_FOX // 2
    lane = lax.broadcasted_iota(jnp.int32, (tq, LANES), 1)
    lo = lane < HEAD_DIM

    def group(ref, ch):
        g = ch % H_DIFF
        return ref[:, g * LANES:(g + 1) * LANES]

    def pieces(x):
        p0, p1, p2 = _split3(x)
        return p0.astype(F32), p1.astype(F32), p2.astype(F32)

    def place3(base, a, x):
        return jnp.where(lane == base, a[0], jnp.where(lane == base + 1, a[1], jnp.where(lane == base + 2, a[2], x)))

    @pl.when(j == 0)
    def _():
        r_local = lax.broadcasted_iota(jnp.int32, (tq, LANES), 0).astype(F32)
        for ch in range(n_chain):
            q = group(qd_ref if ch < H_DIFF else qf_ref, ch)
            q2_ref[ch, :tq, :LANES] = jnp.where(lo, q, jnp.zeros_like(q))
            q2_ref[ch, tq:, :LANES] = jnp.where(lo, jnp.zeros_like(q), q)
            if ch < H_DIFF:
                slope = _alibi_slope(ch)
                x = jnp.where((lane >= X_ALIBI_K) & (lane < X_ALIBI_K + 3), slope, 0.0)
                x = place3(X_ALIBI_Q, pieces(r_local * (-(slope * LOG2E))), x).astype(BF16)
                q2_ref[ch, :tq, LANES:] = x
                q2_ref[ch, tq:, LANES:] = x
            else:
                for e in range(2):
                    hh = 2 * (ch - H_DIFF) + e
                    pick = (lane == hh) | (lane == H_FOX + hh) | (lane == 2 * H_FOX + hh)
                    x = jnp.where(pick, 1.0, 0.0)
                    cq = jnp.broadcast_to(cq_ref[:, hh:hh + 1], (tq, LANES))
                    x = place3(X_FOX_Q, pieces(cq * LOG2E), x)
                    q2_ref[ch, e * tq:(e + 1) * tq, LANES:] = x.astype(BF16)
        m_ref[...] = jnp.full_like(m_ref, NEG)
        l_ref[...] = jnp.zeros_like(l_ref)
        acc_ref[...] = jnp.zeros_like(acc_ref)

    def key_extra():
        b = pieces(ck_ref[...] * (-LOG2E))
        x = jnp.where(lane < H_FOX, b[0], jnp.where(lane < 2 * H_FOX, pltpu.roll(b[1], H_FOX, 1),
                                                    pltpu.roll(b[2], 2 * H_FOX, 1)))
        ones = ((lane >= X_FOX_Q) & (lane < X_FOX_Q + 3)) | ((lane >= X_ALIBI_Q) & (lane < X_ALIBI_Q + 3))
        x = jnp.where(ones, 1.0, x)
        kpos = (lax.broadcasted_iota(jnp.int32, (tq, LANES), 0) + (j - i) * tq).astype(F32)
        return place3(X_ALIBI_K, pieces(kpos * LOG2E), x).astype(BF16)

    def step(masked):
        kx = key_extra()
        if masked:
            rr = lax.broadcasted_iota(jnp.int32, (tq, tq), 0)
            cc = lax.broadcasted_iota(jnp.int32, (tq, tq), 1)
            negmask = jnp.where(rr >= cc, 0.0, NEG)

        def scores(ch):
            kk = jnp.concatenate([group(kd_ref if ch < H_DIFF else kf_ref, ch), kx], axis=1)
            s = _dot_nt(q2_ref[ch], kk)
            if masked:
                s = jnp.concatenate([s[:tq] + negmask, s[tq:] + negmask], axis=0)
            return s

        def update(ch, s):
            v = group(vd_ref if ch < H_DIFF else vf_ref, ch)
            m_prev = m_ref[ch]
            m_new = jnp.maximum(m_prev, jnp.max(s, axis=-1, keepdims=True))
            alpha = jnp.exp2(m_prev - m_new)
            ps = [jnp.exp2(s[:, c * LANES:(c + 1) * LANES] - m_new) for c in range(tq // LANES)]
            psum = ps[0]
            for pc in ps[1:]:
                psum = psum + pc
            l_ref[ch] = alpha * l_ref[ch] + jnp.sum(psum, axis=-1, keepdims=True)
            m_ref[ch] = m_new
            p = jnp.concatenate([pc.astype(BF16) for pc in ps], axis=1)
            acc_ref[ch] = alpha * acc_ref[ch] + _dot(p, v)

        s_next = scores(0)
        for ch in range(n_chain):
            s_cur = s_next
            if ch + 1 < n_chain:
                s_next = scores(ch + 1)
            update(ch, s_cur)

    @pl.when(j < i)
    def _():
        step(False)

    @pl.when(j == i)
    def _():
        step(True)
        lam = _lambda(lq1_ref[...], lk1_ref[...], lq2_ref[...], lk2_ref[...], lam_init)
        for ch in range(n_chain):
            od = acc_ref[ch] / l_ref[ch]
            if ch < H_DIFF:
                o = _rms(od[:tq] - lam * od[tq:], gs_ref[...]) * (1.0 - lam_init)
                o_ref[:, ch * LANES:(ch + 1) * LANES] = o.astype(o_ref.dtype)
            else:
                g = ch - H_DIFF
                o = jnp.where(lo, od[:tq], od[tq:])
                o_ref[:, D_DIFF + g * LANES:D_DIFF + (g + 1) * LANES] = o.astype(o_ref.dtype)


def _prompt_attn(qd, qf, kdb, vdb, kfb, vfb, c, lam_vecs, g_sub, *, batch, seq, lam_init):
    tq = min(TQ, seq)
    nq = seq // tq
    n_chain = H_DIFF + H_FOX // 2
    q_spec = pl.BlockSpec((tq, 512), lambda b, i, j: (b * nq + i, 0))
    k_spec = pl.BlockSpec((tq, 512), lambda b, i, j: (b * nq + jnp.minimum(i, j), 0))
    vec64 = _const_spec((1, HEAD_DIM))
    return pl.pallas_call(
        functools.partial(_prompt_attn_body, tq=tq, lam_init=lam_init),
        grid=(batch, nq, nq),
        in_specs=[q_spec, q_spec, k_spec, k_spec, k_spec, k_spec,
                  pl.BlockSpec((tq, LANES), lambda b, i, j: (b * nq + i, 0)),
                  pl.BlockSpec((tq, LANES), lambda b, i, j: (b * nq + jnp.minimum(i, j), 0)),
                  vec64, vec64, vec64, vec64, _const_spec((1, DIFF_VDIM))],
        out_specs=pl.BlockSpec((tq, D_MIX), lambda b, i, j: (b * nq + i, 0)),
        out_shape=jax.ShapeDtypeStruct((batch * seq, D_MIX), BF16),
        scratch_shapes=[pltpu.VMEM((n_chain, 2 * tq, 2 * LANES), BF16),
                        pltpu.VMEM((n_chain, 2 * tq, LANES), F32), pltpu.VMEM((n_chain, 2 * tq, LANES), F32),
                        pltpu.VMEM((n_chain, 2 * tq, LANES), F32)],
        compiler_params=_cparams(("arbitrary", "arbitrary", "arbitrary")),
        name="prompt_attn",
    )(qd, qf, kdb, vdb, kfb, vfb, c, c, *lam_vecs, g_sub)


def _sample_attn_body(pt_ref, qd_ref, qf_ref, kdn_ref, vdn_ref, kfn_ref, vfn_ref, cn_ref,
                      lq1_ref, lk1_ref, lq2_ref, lk2_ref, gs_ref, *rest, n_pages, lam_init):
    kd_refs = rest[0 * PP:1 * PP]
    vd_refs = rest[1 * PP:2 * PP]
    kf_refs = rest[2 * PP:3 * PP]
    vf_refs = rest[3 * PP:4 * PP]
    lf_refs = rest[4 * PP:5 * PP]
    o_ref, qbd_ref, qbf_ref, md_ref, ld_ref, accd_ref, mf_ref, lf_ref, accf_ref, sa_ref, cqb_ref = rest[5 * PP:]
    g = pl.program_id(1)
    n_steps = n_pages // PP
    rows = 2 * H_DIFF * TPAD

    row_i = lax.broadcasted_iota(jnp.int32, (rows, LANES), 0)
    lane_i = lax.broadcasted_iota(jnp.int32, (rows, LANES), 1)
    t_of_row = row_i % TPAD

    @pl.when(g == 0)
    def _():
        r512 = lax.broadcasted_iota(jnp.int32, (rows, 512), 0)
        l512 = lax.broadcasted_iota(jnp.int32, (rows, 512), 1)
        unit = r512 // TPAD == l512 // HEAD_DIM
        qd8 = jnp.concatenate([qd_ref[0]] * (rows // TPAD), axis=0)
        qf8 = jnp.concatenate([qf_ref[0]] * (rows // TPAD), axis=0)
        qbd_ref[...] = jnp.where(unit, qd8, 0.0).astype(BF16)
        qbf_ref[...] = jnp.where(unit, qf8, 0.0).astype(BF16)
        md_ref[...] = jnp.full_like(md_ref, NEG)
        mf_ref[...] = jnp.full_like(mf_ref, NEG)
        ld_ref[...] = jnp.zeros_like(ld_ref)
        lf_ref[...] = jnp.zeros_like(lf_ref)
        accd_ref[...] = jnp.zeros_like(accd_ref)
        accf_ref[...] = jnp.zeros_like(accf_ref)
        sa_ref[...] = jnp.zeros_like(sa_ref)
        cn = cn_ref[0]
        cqb_ref[...] = jnp.concatenate(
            [jnp.broadcast_to(cn[:, h:h + 1], (TPAD, LANES)) for h in range(H_FOX)], axis=0)

    slope = jnp.zeros((rows, LANES), F32)
    for h in range(H_DIFF):
        slope = jnp.where(row_i // (2 * TPAD) == h, _alibi_slope(h), slope)

    def online(m_ref, l_ref, s):
        m_prev = m_ref[...]
        m_new = jnp.maximum(m_prev, jnp.max(s, axis=-1, keepdims=True))
        alpha = jnp.exp(m_prev - m_new)
        p = jnp.exp(s - m_new)
        l_ref[...] = alpha * l_ref[...] + jnp.sum(p, axis=-1, keepdims=True)
        m_ref[...] = m_new
        return alpha, p.astype(BF16)

    p0 = n_pages - (g + 1) * PP
    ki = lax.broadcasted_iota(jnp.int32, (PAGE, PAGE), 0)
    kj = lax.broadcasted_iota(jnp.int32, (PAGE, PAGE), 1)
    later = (ki > kj).astype(BF16)
    qbd = qbd_ref[...]
    qbf = qbf_ref[...]
    past_len = n_pages * PAGE

    lf_all = jnp.concatenate([lf_refs[r][...] for r in range(PP)], axis=0)
    lf_pieces = _split3(lf_all)
    ones = jnp.ones((PAGE, PAGE), BF16)
    within = _dot(lf_pieces[0], later) + _dot(lf_pieces[1], later) + _dot(lf_pieces[2], later)
    totals = _dot(lf_pieces[0], ones) + _dot(lf_pieces[1], ones) + _dot(lf_pieces[2], ones)

    sd = [None] * PP
    sf = [None] * PP
    s_after = sa_ref[...]
    for r in reversed(range(PP)):
        rt = within[r * H_FOX:(r + 1) * H_FOX] + s_after
        s_after = s_after + totals[r * H_FOX:(r + 1) * H_FOX]
        rt_rows = jnp.concatenate(
            [jnp.broadcast_to(rt[h:h + 1, :], (TPAD, LANES)) for h in range(H_FOX)], axis=0)
        sf[r] = _dot(qbf, kf_refs[r][...].astype(BF16)) + (cqb_ref[...] + rt_rows)
        off = past_len - (p0 + r) * PAGE
        dist = (t_of_row - lane_i + off).astype(F32)
        sd[r] = _dot(qbd, kd_refs[r][...].astype(BF16)) - slope * dist
    sa_ref[...] = s_after

    alpha_d, p_d = online(md_ref, ld_ref, jnp.concatenate(sd, axis=1))
    alpha_f, p_f = online(mf_ref, lf_ref, jnp.concatenate(sf, axis=1))
    pv_heads = []
    for h in range(H_DIFF):
        acc = None
        for r in range(PP):
            v = vd_refs[r][pl.ds(h, PAGE, stride=H_DIFF), :].astype(BF16)
            d = _dot(p_d[h * 2 * TPAD:(h + 1) * 2 * TPAD, r * PAGE:(r + 1) * PAGE], v)
            acc = d if acc is None else acc + d
        pv_heads.append(acc)
    accd_ref[...] = alpha_d * accd_ref[...] + jnp.concatenate(pv_heads, axis=0)

    acc = None
    for r in range(PP):
        d = _dot_nt(p_f[:, r * PAGE:(r + 1) * PAGE], vf_refs[r][...].astype(BF16))
        acc = d if acc is None else acc + d
    accf_ref[...] = alpha_f * accf_ref[...] + acc

    @pl.when(g == n_steps - 1)
    def _():
        rr = lax.broadcasted_iota(jnp.int32, (rows, TPAD), 0) % TPAD
        ss = lax.broadcasted_iota(jnp.int32, (rows, TPAD), 1)
        causal = ss <= rr
        dist_n = (rr - ss).astype(F32)

        s = _dot_nt(qbd, kdn_ref[0].astype(BF16)) - slope[:, :TPAD] * dist_n
        alpha, p = online(md_ref, ld_ref, jnp.where(causal, s, NEG))
        vdn = vdn_ref[0].astype(BF16)
        pv = jnp.concatenate(
            [_dot(p[h * 2 * TPAD:(h + 1) * 2 * TPAD, :], vdn[:, h * LANES:(h + 1) * LANES])
             for h in range(H_DIFF)], axis=0)
        accd = alpha * accd_ref[...] + pv

        cn = cn_ref[0]
        cnt = cn.T
        bias = jnp.concatenate(
            [cn[:, h:h + 1] - cnt[h:h + 1, :] for h in range(H_FOX)], axis=0)
        s = _dot_nt(qbf, kfn_ref[0].astype(BF16)) + bias
        alpha, p = online(mf_ref, lf_ref, jnp.where(causal, s, NEG))
        accf = alpha * accf_ref[...] + _dot(p, vfn_ref[0].astype(BF16))

        lam = _lambda(lq1_ref[...], lk1_ref[...], lq2_ref[...], lk2_ref[...], lam_init)
        od = accd / ld_ref[...]
        for h in range(H_DIFF):
            od0 = od[(2 * h) * TPAD:(2 * h + 1) * TPAD]
            od1 = od[(2 * h + 1) * TPAD:(2 * h + 2) * TPAD]
            o = _rms(od0 - lam * od1, gs_ref[...]) * (1.0 - lam_init)
            o_ref[0, :, h * LANES:(h + 1) * LANES] = o
        of = accf / lf_ref[...]
        l512 = lax.broadcasted_iota(jnp.int32, (TPAD, D_FOX), 1)
        out_f = jnp.zeros((TPAD, D_FOX), F32)
        for h in range(H_FOX):
            out_f = jnp.where(l512 // HEAD_DIM == h, of[h * TPAD:(h + 1) * TPAD], out_f)
        o_ref[0, :, D_DIFF:] = out_f


def _sample_attn(page_table, qd, qf, kdn, vdn, kfn, vfn, cn, lam_vecs, g_sub,
                 kd_pages, vd_pages, kf_pages, vf_pages, lf_pages, *, layer, lam_init):
    batch, n_pages = page_table.shape
    assert n_pages % PP == 0
    n_steps = n_pages // PP
    rows = 2 * H_DIFF * TPAD

    def tok_spec(width):
        return pl.BlockSpec((1, TPAD, width), lambda b, g, pt: (b, 0, 0))

    def page_spec(r, nrow):
        def imap(b, g, pt):
            return (layer, pt[b * n_pages + n_pages - (g + 1) * PP + r], 0, 0)
        return pl.BlockSpec((None, None, nrow, LANES), imap)

    vec64 = pl.BlockSpec((1, HEAD_DIM), lambda b, g, pt: (0, 0))
    in_specs = [tok_spec(512), tok_spec(512), tok_spec(512), tok_spec(512), tok_spec(512), tok_spec(512),
                tok_spec(LANES), vec64, vec64, vec64, vec64,
                pl.BlockSpec((1, DIFF_VDIM), lambda b, g, pt: (0, 0))]
    args = [qd, qf, kdn, vdn, kfn, vfn, cn, *lam_vecs, g_sub]
    for arr, nrow in ((kd_pages, 512), (vd_pages, 512), (kf_pages, 512), (vf_pages, 512), (lf_pages, H_FOX)):
        for r in range(PP):
            in_specs.append(page_spec(r, nrow))
            args.append(arr)
    grid_spec = pltpu.PrefetchScalarGridSpec(
        num_scalar_prefetch=1,
        grid=(batch, n_steps),
        in_specs=in_specs,
        out_specs=pl.BlockSpec((1, TPAD, D_MIX), lambda b, g, pt: (b, 0, 0)),
        scratch_shapes=[pltpu.VMEM((rows, 512), BF16), pltpu.VMEM((rows, 512), BF16),
                        pltpu.VMEM((rows, 1), F32), pltpu.VMEM((rows, 1), F32), pltpu.VMEM((rows, DIFF_VDIM), F32),
                        pltpu.VMEM((rows, 1), F32), pltpu.VMEM((rows, 1), F32), pltpu.VMEM((rows, D_FOX), F32),
                        pltpu.VMEM((H_FOX, LANES), F32), pltpu.VMEM((rows, LANES), F32)],
    )
    return pl.pallas_call(
        functools.partial(_sample_attn_body, n_pages=n_pages, lam_init=lam_init),
        grid_spec=grid_spec,
        out_shape=jax.ShapeDtypeStruct((batch, TPAD, D_MIX), F32),
        compiler_params=_cparams(("arbitrary", "arbitrary")),
        name="sample_attn",
    )(page_table.reshape(-1), *args)


def _prep_ffn(wg, wu, wd):
    d, dff = wg.shape
    n_chunks = dff // TF
    wg_c = wg.astype(BF16).reshape(d, n_chunks, TF).transpose(1, 0, 2)
    wu_c = wu.astype(BF16).reshape(d, n_chunks, TF).transpose(1, 0, 2)
    wd_c = wd.astype(BF16).reshape(n_chunks, TF, d)
    return wg_c, wu_c, wd_c


def _tile_gain(g):
    return jnp.tile(g.astype(F32), 512 // HEAD_DIM).reshape(1, 512)


def _pad_t(a, t):
    b = a.shape[0] // t
    return jnp.pad(a.reshape(b, t, a.shape[1]), ((0, 0), (0, TPAD - t), (0, 0)))


def kernel(x_prompt, x_sample, cache_k_diff, cache_v_diff, cache_k_fox, cache_v_fox, cache_logf_fox, page_table, g_ffn1, w1_gate, w1_up, w1_down, g_attn, w_in, b_f, g_qd, g_kd, g_qf, g_kf, lam_q1, lam_k1, lam_q2, lam_k2, g_subln, w_out, g_ffn2, w2_gate, w2_up, w2_down):
    batch, seq, d = x_prompt.shape
    dbatch, dseq, _ = x_sample.shape
    depth = w_in.shape[0]
    n_pool = cache_k_diff.shape[1]

    kd_pages = jnp.transpose(cache_k_diff, (0, 1, 3, 4, 5, 2)).reshape(depth, n_pool, 512, PAGE)
    vd_pages = cache_v_diff.reshape(depth, n_pool, PAGE * H_DIFF, DIFF_VDIM)
    kf_pages = jnp.transpose(cache_k_fox, (0, 1, 3, 4, 2)).reshape(depth, n_pool, 512, PAGE)
    vf_pages = jnp.transpose(cache_v_fox, (0, 1, 3, 4, 2)).reshape(depth, n_pool, 512, PAGE)
    lf_pages = jnp.transpose(cache_logf_fox, (0, 1, 3, 2))

    xp = x_prompt.reshape(batch * seq, d)
    xs = x_sample.reshape(dbatch * dseq, d)
    outs_p = [[] for _ in range(5)]
    outs_s = [[] for _ in range(5)]
    for l in range(depth):
        lam_init = 0.8 - 0.6 * math.exp(-0.3 * l)
        ffn1 = _prep_ffn(w1_gate[l], w1_up[l], w1_down[l])
        ffn2 = _prep_ffn(w2_gate[l], w2_up[l], w2_down[l])
        win_main = w_in[l][:, :N_MAIN].astype(BF16)
        wf_pad = jnp.pad(w_in[l][:, N_MAIN:], ((0, 0), (0, LANES - H_FOX))).astype(BF16)
        bf_pad = jnp.pad(b_f[l].astype(F32), (0, LANES - H_FOX)).reshape(1, LANES)
        gains = (_tile_gain(g_qd[l]), _tile_gain(g_kd[l]), _tile_gain(g_qf[l]), _tile_gain(g_kf[l]))
        wo = w_out[l].astype(BF16)
        lam_vecs = tuple(v[l].astype(F32).reshape(1, HEAD_DIM) for v in (lam_q1, lam_k1, lam_q2, lam_k2))
        g_sub = g_subln[l].astype(F32).reshape(1, DIFF_VDIM)

        hp = _ffn(xp, g_ffn1[l], *ffn1)
        (qd, qf, kd, vd, kf, vf, lf, kdb, vdb, kfb, vfb, c) = _proj(
            hp, g_attn[l], win_main, wf_pad, bf_pad, *gains, seg=seq, q_dtype=BF16,
            q_scale=ATTN_SCALE * LOG2E, native=True)
        mix = _prompt_attn(qd, qf, kdb, vdb, kfb, vfb, c, lam_vecs, g_sub,
                           batch=batch, seq=seq, lam_init=lam_init)
        xp = _ffn(hp, g_ffn2[l], *ffn2, mix=mix, w_out=wo)
        for lst, a in zip(outs_p, (kd, vd, kf, vf, lf)):
            lst.append(a)

        hs = _ffn(xs, g_ffn1[l], *ffn1)
        (qd, qf, kd, vd, kf, vf, lf, _, _, _, _, c) = _proj(
            hs, g_attn[l], win_main, wf_pad, bf_pad, *gains, seg=dseq, q_dtype=F32,
            q_scale=ATTN_SCALE, native=False)
        mix = _sample_attn(page_table, _pad_t(qd, dseq), _pad_t(qf, dseq), _pad_t(kd, dseq), _pad_t(vd, dseq),
                           _pad_t(kf, dseq), _pad_t(vf, dseq), _pad_t(c, dseq), lam_vecs, g_sub,
                           kd_pages, vd_pages, kf_pages, vf_pages, lf_pages, layer=l, lam_init=lam_init)
        mix = mix[:, :dseq].reshape(dbatch * dseq, D_MIX)
        xs = _ffn(hs, g_ffn2[l], *ffn2, mix=mix, w_out=wo)
        for lst, a in zip(outs_s, (kd, vd, kf, vf, lf)):
            lst.append(a)

    def stack(lst, b, t, tail):
        return jnp.stack(lst).reshape(depth, b, t, *tail)

    tails = ((H_DIFF, 2, HEAD_DIM), (H_DIFF, DIFF_VDIM), (H_FOX, HEAD_DIM), (H_FOX, HEAD_DIM), (H_FOX,))
    kd_p, vd_p, kf_p, vf_p, lf_p = (jnp.stack(lst) for lst in outs_p)
    new_p = (
        kd_p.reshape(depth, batch, H_DIFF, 2, HEAD_DIM, seq).transpose(0, 1, 5, 2, 3, 4),
        vd_p.reshape(depth, batch, seq, H_DIFF, DIFF_VDIM),
        kf_p.reshape(depth, batch, H_FOX, HEAD_DIM, seq).transpose(0, 1, 4, 2, 3),
        vf_p.reshape(depth, batch, H_FOX, HEAD_DIM, seq).transpose(0, 1, 4, 2, 3),
        lf_p.transpose(0, 1, 3, 2),
    )
    return (xp.reshape(batch, seq, d), xs.reshape(dbatch, dseq, d), *new_p,
            *[stack(lst, dbatch, dseq, tail) for lst, tail in zip(outs_s, tails)])
```

```python
import functools
import math

import jax
import jax.numpy as jnp
from jax import lax
from jax.experimental import pallas as pl
from jax.experimental.pallas import tpu as pltpu

F32 = jnp.float32
BF16 = jnp.bfloat16

HEAD_DIM = 64
H_DIFF = 4
H_FOX = 8
DIFF_VDIM = 2 * HEAD_DIM
D_DIFF = H_DIFF * DIFF_VDIM
D_FOX = H_FOX * HEAD_DIM
D_MIX = D_DIFF + D_FOX
PAGE = 128
EPS = 1e-6
NEG = -1e30
ATTN_SCALE = HEAD_DIM ** -0.5
LOG2E = math.log2(math.e)
N_MAIN = 6 * 512

LANES = 128
SUBLANES = 8
VMEM_LIMIT = 56 * 1024 * 1024

TM = 512
TF = 256
TQ = 512
PP = 8
N_SLOT = 3
TPAD = 8


def _cparams(sem):
    return pltpu.CompilerParams(dimension_semantics=sem, vmem_limit_bytes=VMEM_LIMIT)


def _const_spec(shape):
    nd = len(shape)
    return pl.BlockSpec(shape, lambda *_: (0,) * nd, pipeline_mode=pl.Buffered(1))


def _rms(x, g):
    return x * lax.rsqrt(jnp.mean(x * x, axis=-1, keepdims=True) + EPS) * g


def _split3(x):
    p0 = x.astype(BF16)
    r1 = x - p0.astype(F32)
    p1 = r1.astype(BF16)
    p2 = (r1 - p1.astype(F32)).astype(BF16)
    return p0, p1, p2


def _dot(a, b):
    return jnp.dot(a, b, preferred_element_type=F32)


def _dot_nt(a, b):
    return lax.dot_general(a, b, (((1,), (1,)), ((), ())), preferred_element_type=F32)


def _exact_dot(t, x):
    p0, p1, p2 = _split3(x)
    return _dot(t, p0) + _dot(t, p1) + _dot(t, p2)


def _exact_dot_r(x, t):
    p0, p1, p2 = _split3(x)
    return _dot(p0, t) + _dot(p1, t) + _dot(p2, t)


def _ffn_body(*refs, tf, fuse_out):
    if fuse_out:
        x_ref, mix_ref, wo_ref, g_ref, wg_ref, wu_ref, wd_ref, o_ref, acc_ref, u_ref = refs
        x = x_ref[...] + _dot(mix_ref[...].astype(BF16), wo_ref[...])
    else:
        x_ref, g_ref, wg_ref, wu_ref, wd_ref, o_ref, acc_ref, u_ref = refs
        x = x_ref[...]
    o_ref[...] = x
    u_ref[...] = _rms(x, g_ref[...]).astype(BF16)
    n_chunks = wg_ref.shape[1] // tf

    def gate_up(c):
        u = u_ref[...]
        return _dot(u, wg_ref[:, c * tf:(c + 1) * tf]), _dot(u, wu_ref[:, c * tf:(c + 1) * tf])

    nxt = gate_up(0)
    for c in range(n_chunks):
        gate, up = nxt
        if c + 1 < n_chunks:
            nxt = gate_up(c + 1)
        a = (gate * jax.nn.sigmoid(gate) * up).astype(BF16)
        d = _dot(a, wd_ref[c * tf:(c + 1) * tf, :])
        if c == 0:
            acc_ref[...] = d
        else:
            acc_ref[...] += d
    o_ref[...] = o_ref[...] + 0.5 * acc_ref[...]


def _ffn(x, g, wg_c, wu_c, wd_c, mix=None, w_out=None):
    n, d = x.shape
    assert wg_c.shape[1] % TF == 0
    tm = min(TM, n)
    fuse = mix is not None
    tok = pl.BlockSpec((tm, d), lambda i: (i, 0))
    in_specs = [tok]
    args = [x]
    if fuse:
        in_specs += [pl.BlockSpec((tm, mix.shape[1]), lambda i: (i, 0)), _const_spec(w_out.shape)]
        args += [mix, w_out]
    in_specs += [_const_spec((1, d)), _const_spec(wg_c.shape), _const_spec(wu_c.shape), _const_spec(wd_c.shape)]
    args += [g.reshape(1, d), wg_c, wu_c, wd_c]
    return pl.pallas_call(
        functools.partial(_ffn_body, tf=TF, fuse_out=fuse),
        grid=(n // tm,),
        in_specs=in_specs,
        out_specs=tok,
        out_shape=jax.ShapeDtypeStruct((n, d), F32),
        scratch_shapes=[pltpu.VMEM((tm, d), F32), pltpu.VMEM((tm, d), BF16)],
        compiler_params=_cparams(("arbitrary",)),
        name="ffn_out" if fuse else "ffn",
    )(*args)


def _proj_body(h_ref, ga_ref, win_ref, wf_ref, bf_ref, gqd_ref, gkd_ref, gqf_ref, gkf_ref,
               qd_o, qf_o, kd_o, vd_o, kf_o, vf_o, lf_o, kdb_o, vdb_o, kfb_o, vfb_o, c_o,
               carry_ref, *, tm, seg, q_scale, native):
    i = pl.program_id(0)
    u = _rms(h_ref[...], ga_ref[...]).astype(BF16)

    def put_rows(o_ref, x):
        o_ref[...] = x.T if native else x

    r = lax.broadcasted_iota(jnp.int32, (2 * LANES, 2 * LANES), 0)
    c = lax.broadcasted_iota(jnp.int32, (2 * LANES, 2 * LANES), 1)
    group = (r // HEAD_DIM == c // HEAD_DIM).astype(BF16)

    def head_norm(x, g):
        x2 = (x * x).astype(BF16)
        ss = jnp.concatenate([_dot(x2[:, :2 * LANES], group), _dot(x2[:, 2 * LANES:], group)], axis=1)
        return x * lax.rsqrt(ss * (1.0 / HEAD_DIM) + EPS) * g

    def seg_cols(k):
        return _dot(u, win_ref[:, k * 512:(k + 1) * 512])

    qd_o[...] = (head_norm(seg_cols(0), gqd_ref[...]) * q_scale).astype(qd_o.dtype)
    kd = head_norm(seg_cols(1), gkd_ref[...])
    put_rows(kd_o, kd)
    kdb_o[...] = kd.astype(BF16)
    vd = seg_cols(2)
    if native:
        for h in range(H_DIFF):
            vd_o[pl.ds(h, tm, stride=H_DIFF), :] = vd[:, h * DIFF_VDIM:(h + 1) * DIFF_VDIM]
    else:
        vd_o[...] = vd
    vdb_o[...] = vd.astype(BF16)
    qf_o[...] = (head_norm(seg_cols(3), gqf_ref[...]) * q_scale).astype(qf_o.dtype)
    kf = head_norm(seg_cols(4), gkf_ref[...])
    put_rows(kf_o, kf)
    kfb_o[...] = kf.astype(BF16)
    vf = seg_cols(5)
    put_rows(vf_o, vf)
    vfb_o[...] = vf.astype(BF16)

    lane = lax.broadcasted_iota(jnp.int32, (tm, LANES), 1)
    logf = jnp.where(lane < H_FOX, jax.nn.log_sigmoid(_dot(u, wf_ref[...]) + bf_ref[...]), 0.0)
    lf_o[...] = logf.T[:H_FOX, :] if native else logf[:, :H_FOX]

    ri = lax.broadcasted_iota(jnp.int32, (tm, tm), 0)
    ci = lax.broadcasted_iota(jnp.int32, (tm, tm), 1)
    if seg >= tm:
        tri = (ci <= ri).astype(BF16)
        tiles_per_seg = seg // tm

        @pl.when(i % tiles_per_seg == 0)
        def _():
            carry_ref[...] = jnp.zeros_like(carry_ref)

        csum = _exact_dot(tri, logf) + carry_ref[0:1, :]
        carry_ref[...] = jnp.broadcast_to(csum[tm - 1:tm, :], carry_ref.shape)
    else:
        tri = ((ci <= ri) & (ri // seg == ci // seg)).astype(BF16)
        csum = _exact_dot(tri, logf)
    c_o[...] = csum


def _proj(h, ga, win_main, wf_pad, bf_pad, gqd, gkd, gqf, gkf, *, seg, q_dtype, q_scale, native):
    n, d = h.shape
    tm = min(TM, n)
    tok512 = pl.BlockSpec((tm, 512), lambda i: (i, 0))
    if native:
        assert seg % tm == 0
        tps = seg // tm
        nseq = n // seg
        fm_shape = jax.ShapeDtypeStruct((nseq, 512, seg), F32)
        fm_spec = pl.BlockSpec((None, 512, tm), lambda i: (i // tps, 0, i % tps))
        k_shapes = (fm_shape, jax.ShapeDtypeStruct((n * H_DIFF, DIFF_VDIM), F32), fm_shape, fm_shape,
                    jax.ShapeDtypeStruct((nseq, H_FOX, seg), F32))
        k_specs = (fm_spec, pl.BlockSpec((tm * H_DIFF, DIFF_VDIM), lambda i: (i, 0)), fm_spec, fm_spec,
                   pl.BlockSpec((None, H_FOX, tm), lambda i: (i // tps, 0, i % tps)))
    else:
        k_shapes = (jax.ShapeDtypeStruct((n, 512), F32),) * 4 + (jax.ShapeDtypeStruct((n, H_FOX), F32),)
        k_specs = (tok512,) * 4 + (pl.BlockSpec((tm, H_FOX), lambda i: (i, 0)),)
    out_shape = (
        jax.ShapeDtypeStruct((n, 512), q_dtype), jax.ShapeDtypeStruct((n, 512), q_dtype), *k_shapes,
        jax.ShapeDtypeStruct((n, 512), BF16), jax.ShapeDtypeStruct((n, 512), BF16),
        jax.ShapeDtypeStruct((n, 512), BF16), jax.ShapeDtypeStruct((n, 512), BF16),
        jax.ShapeDtypeStruct((n, LANES), F32),
    )
    out_specs = (tok512, tok512, *k_specs, tok512, tok512, tok512, tok512,
                 pl.BlockSpec((tm, LANES), lambda i: (i, 0)))
    return pl.pallas_call(
        functools.partial(_proj_body, tm=tm, seg=seg, q_scale=q_scale, native=native),
        grid=(n // tm,),
        in_specs=[pl.BlockSpec((tm, d), lambda i: (i, 0)), _const_spec((1, d)),
                  _const_spec(win_main.shape), _const_spec(wf_pad.shape), _const_spec((1, LANES)),
                  _const_spec((1, 512)), _const_spec((1, 512)), _const_spec((1, 512)), _const_spec((1, 512))],
        out_specs=out_specs,
        out_shape=out_shape,
        scratch_shapes=[pltpu.VMEM((SUBLANES, LANES), F32)],
        compiler_params=_cparams(("arbitrary",)),
        name="proj",
    )(h, ga.reshape(1, d), win_main, wf_pad, bf_pad, gqd, gkd, gqf, gkf)


def _lambda(lq1, lk1, lq2, lk2, lam_init):
    a = jnp.sum(lq1 * lk1, axis=-1, keepdims=True)
    b = jnp.sum(lq2 * lk2, axis=-1, keepdims=True)
    return jnp.exp(a) - jnp.exp(b) + lam_init


def _alibi_slope(h):
    return 2.0 ** (-8.0 * (h + 1) / H_DIFF)


X_FOX_K = 0
X_FOX_Q = 24
X_ALIBI_Q = 32
X_ALIBI_K = 35


def _prompt_attn_body(qd_ref, qf_ref, kd_ref, vd_ref, kf_ref, vf_ref, cq_ref, ck_ref,
                      lq1_ref, lk1_ref, lq2_ref, lk2_ref, gs_ref, o_ref,
                      q2_ref, m_ref, l_ref, acc_ref, *, tq, lam_init):
    i = pl.program_id(1)
    j = pl.program_id(2)
    n_chain = H_DIFF + H_FOX // 2
    lane = lax.broadcasted_iota(jnp.int32, (tq, LANES), 1)
    lo = lane < HEAD_DIM

    def group(ref, ch):
        g = ch % H_DIFF
        return ref[:, g * LANES:(g + 1) * LANES]

    def pieces(x):
        p0, p1, p2 = _split3(x)
        return p0.astype(F32), p1.astype(F32), p2.astype(F32)

    def place3(base, a, x):
        return jnp.where(lane == base, a[0], jnp.where(lane == base + 1, a[1], jnp.where(lane == base + 2, a[2], x)))

    @pl.when(j == 0)
    def _():
        r_local = lax.broadcasted_iota(jnp.int32, (tq, LANES), 0).astype(F32)
        for ch in range(n_chain):
            q = group(qd_ref if ch < H_DIFF else qf_ref, ch)
            q2_ref[ch, :tq, :LANES] = jnp.where(lo, q, jnp.zeros_like(q))
            q2_ref[ch, tq:, :LANES] = jnp.where(lo, jnp.zeros_like(q), q)
            if ch < H_DIFF:
                slope = _alibi_slope(ch)
                x = jnp.where((lane >= X_ALIBI_K) & (lane < X_ALIBI_K + 3), slope, 0.0)
                x = place3(X_ALIBI_Q, pieces(r_local * (-(slope * LOG2E))), x).astype(BF16)
                q2_ref[ch, :tq, LANES:] = x
                q2_ref[ch, tq:, LANES:] = x
            else:
                for e in range(2):
                    hh = 2 * (ch - H_DIFF) + e
                    pick = (lane == hh) | (lane == H_FOX + hh) | (lane == 2 * H_FOX + hh)
                    x = jnp.where(pick, 1.0, 0.0)
                    cq = jnp.broadcast_to(cq_ref[:, hh:hh + 1], (tq, LANES))
                    x = place3(X_FOX_Q, pieces(cq * LOG2E), x)
                    q2_ref[ch, e * tq:(e + 1) * tq, LANES:] = x.astype(BF16)
        m_ref[...] = jnp.full_like(m_ref, NEG)
        l_ref[...] = jnp.zeros_like(l_ref)
        acc_ref[...] = jnp.zeros_like(acc_ref)

    def key_extra():
        b = pieces(ck_ref[...] * (-LOG2E))
        x = jnp.where(lane < H_FOX, b[0], jnp.where(lane < 2 * H_FOX, pltpu.roll(b[1], H_FOX, 1),
                                                    pltpu.roll(b[2], 2 * H_FOX, 1)))
        ones = ((lane >= X_FOX_Q) & (lane < X_FOX_Q + 3)) | ((lane >= X_ALIBI_Q) & (lane < X_ALIBI_Q + 3))
        x = jnp.where(ones, 1.0, x)
        kpos = (lax.broadcasted_iota(jnp.int32, (tq, LANES), 0) + (j - i) * tq).astype(F32)
        return place3(X_ALIBI_K, pieces(kpos * LOG2E), x).astype(BF16)

    def step(masked):
        kx = key_extra()
        if masked:
            rr = lax.broadcasted_iota(jnp.int32, (tq, tq), 0)
            cc = lax.broadcasted_iota(jnp.int32, (tq, tq), 1)
            negmask = jnp.where(rr >= cc, 0.0, NEG)

        def scores(ch):
            kk = jnp.concatenate([group(kd_ref if ch < H_DIFF else kf_ref, ch), kx], axis=1)
            s = _dot_nt(q2_ref[ch], kk)
            if masked:
                s = jnp.concatenate([s[:tq] + negmask, s[tq:] + negmask], axis=0)
            return s

        def update(ch, s):
            v = group(vd_ref if ch < H_DIFF else vf_ref, ch)
            m_prev = m_ref[ch]
            m_new = jnp.maximum(m_prev, jnp.max(s, axis=-1, keepdims=True))
            alpha = jnp.exp2(m_prev - m_new)
            ps = [jnp.exp2(s[:, c * LANES:(c + 1) * LANES] - m_new) for c in range(tq // LANES)]
            psum = ps[0]
            for pc in ps[1:]:
                psum = psum + pc
            l_ref[ch] = alpha * l_ref[ch] + jnp.sum(psum, axis=-1, keepdims=True)
            m_ref[ch] = m_new
            p = jnp.concatenate([pc.astype(BF16) for pc in ps], axis=1)
            acc_ref[ch] = alpha * acc_ref[ch] + _dot(p, v)

        s_next = scores(0)
        for ch in range(n_chain):
            s_cur = s_next
            if ch + 1 < n_chain:
                s_next = scores(ch + 1)
            update(ch, s_cur)

    @pl.when(j < i)
    def _():
        step(False)

    @pl.when(j == i)
    def _():
        step(True)
        lam = _lambda(lq1_ref[...], lk1_ref[...], lq2_ref[...], lk2_ref[...], lam_init)
        for ch in range(n_chain):
            od = acc_ref[ch] / l_ref[ch]
            if ch < H_DIFF:
                o = _rms(od[:tq] - lam * od[tq:], gs_ref[...]) * (1.0 - lam_init)
                o_ref[:, ch * LANES:(ch + 1) * LANES] = o.astype(o_ref.dtype)
            else:
                g = ch - H_DIFF
                o = jnp.where(lo, od[:tq], od[tq:])
                o_ref[:, D_DIFF + g * LANES:D_DIFF + (g + 1) * LANES] = o.astype(o_ref.dtype)


def _prompt_attn(qd, qf, kdb, vdb, kfb, vfb, c, lam_vecs, g_sub, *, batch, seq, lam_init):
    tq = min(TQ, seq)
    nq = seq // tq
    n_chain = H_DIFF + H_FOX // 2
    q_spec = pl.BlockSpec((tq, 512), lambda b, i, j: (b * nq + i, 0))
    k_spec = pl.BlockSpec((tq, 512), lambda b, i, j: (b * nq + jnp.minimum(i, j), 0))
    vec64 = _const_spec((1, HEAD_DIM))
    return pl.pallas_call(
        functools.partial(_prompt_attn_body, tq=tq, lam_init=lam_init),
        grid=(batch, nq, nq),
        in_specs=[q_spec, q_spec, k_spec, k_spec, k_spec, k_spec,
                  pl.BlockSpec((tq, LANES), lambda b, i, j: (b * nq + i, 0)),
                  pl.BlockSpec((tq, LANES), lambda b, i, j: (b * nq + jnp.minimum(i, j), 0)),
                  vec64, vec64, vec64, vec64, _const_spec((1, DIFF_VDIM))],
        out_specs=pl.BlockSpec((tq, D_MIX), lambda b, i, j: (b * nq + i, 0)),
        out_shape=jax.ShapeDtypeStruct((batch * seq, D_MIX), BF16),
        scratch_shapes=[pltpu.VMEM((n_chain, 2 * tq, 2 * LANES), BF16),
                        pltpu.VMEM((n_chain, 2 * tq, LANES), F32), pltpu.VMEM((n_chain, 2 * tq, LANES), F32),
                        pltpu.VMEM((n_chain, 2 * tq, LANES), F32)],
        compiler_params=_cparams(("arbitrary", "arbitrary", "arbitrary")),
        name="prompt_attn",
    )(qd, qf, kdb, vdb, kfb, vfb, c, c, *lam_vecs, g_sub)


def _sample_attn_body(pt_ref, qd_ref, qf_ref, kdn_ref, vdn_ref, kfn_ref, vfn_ref, cn_ref,
                      lq1_ref, lk1_ref, lq2_ref, lk2_ref, gs_ref,
                      kd_hbm, vd_hbm, kf_hbm, vf_hbm, lfp_hbm, o_ref,
                      qbd_ref, qbf_ref, md_ref, ld_ref, accd_ref, mf_ref, lf_ref, accf_ref, sa_ref, cqb_ref,
                      kd_buf, vd_buf, kf_buf, vf_buf, lfp_buf, sem, *, layer, batch, n_pages, lam_init):
    b = pl.program_id(0)
    g = pl.program_id(1)
    n_steps = n_pages // PP
    rows = 2 * H_DIFF * TPAD

    step = b * n_steps + g
    n_total = batch * n_steps
    streams = ((kd_hbm, kd_buf), (vd_hbm, vd_buf), (kf_hbm, kf_buf), (vf_hbm, vf_buf), (lfp_hbm, lfp_buf))

    def page_copy(k, page, slot, r):
        hbm, buf = streams[k]
        return pltpu.make_async_copy(hbm.at[layer, page], buf.at[slot, r], sem.at[slot, k, r])

    def fetch(s):
        sb = s // n_steps
        base = sb * n_pages + n_pages - (s % n_steps + 1) * PP
        slot = s % N_SLOT
        for r in range(PP):
            page = pt_ref[base + r]
            for k in range(len(streams)):
                page_copy(k, page, slot, r).start()

    @pl.when(step == 0)
    def _():
        fetch(step)
        if n_total > 1:
            fetch(step + 1)

    @pl.when(step + 2 < n_total)
    def _():
        fetch(step + 2)

    cur = step % N_SLOT
    for r in range(PP):
        for k in range(len(streams)):
            page_copy(k, 0, cur, r).wait()
    kd_refs = [kd_buf.at[cur, r] for r in range(PP)]
    vd_refs = [vd_buf.at[cur, r] for r in range(PP)]
    kf_refs = [kf_buf.at[cur, r] for r in range(PP)]
    vf_refs = [vf_buf.at[cur, r] for r in range(PP)]
    lf_refs = [lfp_buf.at[cur, r] for r in range(PP)]

    row_i = lax.broadcasted_iota(jnp.int32, (rows, LANES), 0)
    lane_i = lax.broadcasted_iota(jnp.int32, (rows, LANES), 1)
    t_of_row = row_i % TPAD

    @pl.when(g == 0)
    def _():
        r512 = lax.broadcasted_iota(jnp.int32, (rows, 512), 0)
        l512 = lax.broadcasted_iota(jnp.int32, (rows, 512), 1)
        unit = r512 // TPAD == l512 // HEAD_DIM
        qd8 = jnp.concatenate([qd_ref[0]] * (rows // TPAD), axis=0)
        qf8 = jnp.concatenate([qf_ref[0]] * (rows // TPAD), axis=0)
        qbd_ref[...] = jnp.where(unit, qd8, 0.0).astype(BF16)
        qbf_ref[...] = jnp.where(unit, qf8, 0.0).astype(BF16)
        md_ref[...] = jnp.full_like(md_ref, NEG)
        mf_ref[...] = jnp.full_like(mf_ref, NEG)
        ld_ref[...] = jnp.zeros_like(ld_ref)
        lf_ref[...] = jnp.zeros_like(lf_ref)
        accd_ref[...] = jnp.zeros_like(accd_ref)
        accf_ref[...] = jnp.zeros_like(accf_ref)
        sa_ref[...] = jnp.zeros_like(sa_ref)
        cn = cn_ref[0]
        cqb_ref[...] = jnp.concatenate(
            [jnp.broadcast_to(cn[:, h:h + 1], (TPAD, LANES)) for h in range(H_FOX)], axis=0)

    slope = jnp.zeros((rows, LANES), F32)
    for h in range(H_DIFF):
        slope = jnp.where(row_i // (2 * TPAD) == h, _alibi_slope(h), slope)

    def online(m_ref, l_ref, s):
        m_prev = m_ref[...]
        m_new = jnp.maximum(m_prev, jnp.max(s, axis=-1, keepdims=True))
        alpha = jnp.exp(m_prev - m_new)
        p = jnp.exp(s - m_new)
        l_ref[...] = alpha * l_ref[...] + jnp.sum(p, axis=-1, keepdims=True)
        m_ref[...] = m_new
        return alpha, p.astype(BF16)

    p0 = n_pages - (g + 1) * PP
    ki = lax.broadcasted_iota(jnp.int32, (PAGE, PAGE), 0)
    kj = lax.broadcasted_iota(jnp.int32, (PAGE, PAGE), 1)
    later = (ki > kj).astype(BF16)
    qbd = qbd_ref[...]
    qbf = qbf_ref[...]
    past_len = n_pages * PAGE

    lf_all = jnp.concatenate([lf_refs[r][...] for r in range(PP)], axis=0)
    lf_pieces = _split3(lf_all)
    ones = jnp.ones((PAGE, PAGE), BF16)
    within = _dot(lf_pieces[0], later) + _dot(lf_pieces[1], later) + _dot(lf_pieces[2], later)
    totals = _dot(lf_pieces[0], ones) + _dot(lf_pieces[1], ones) + _dot(lf_pieces[2], ones)

    sd = [None] * PP
    sf = [None] * PP
    s_after = sa_ref[...]
    for r in reversed(range(PP)):
        rt = within[r * H_FOX:(r + 1) * H_FOX] + s_after
        s_after = s_after + totals[r * H_FOX:(r + 1) * H_FOX]
        rt_rows = jnp.concatenate(
            [jnp.broadcast_to(rt[h:h + 1, :], (TPAD, LANES)) for h in range(H_FOX)], axis=0)
        sf[r] = _dot(qbf, kf_refs[r][...].astype(BF16)) + (cqb_ref[...] + rt_rows)
        off = past_len - (p0 + r) * PAGE
        dist = (t_of_row - lane_i + off).astype(F32)
        sd[r] = _dot(qbd, kd_refs[r][...].astype(BF16)) - slope * dist
    sa_ref[...] = s_after

    alpha_d, p_d = online(md_ref, ld_ref, jnp.concatenate(sd, axis=1))
    alpha_f, p_f = online(mf_ref, lf_ref, jnp.concatenate(sf, axis=1))
    pv_heads = []
    for h in range(H_DIFF):
        acc = None
        for r in range(PP):
            v = vd_refs[r][pl.ds(h, PAGE, stride=H_DIFF), :].astype(BF16)
            d = _dot(p_d[h * 2 * TPAD:(h + 1) * 2 * TPAD, r * PAGE:(r + 1) * PAGE], v)
            acc = d if acc is None else acc + d
        pv_heads.append(acc)
    accd_ref[...] = alpha_d * accd_ref[...] + jnp.concatenate(pv_heads, axis=0)

    acc = None
    for r in range(PP):
        d = _dot_nt(p_f[:, r * PAGE:(r + 1) * PAGE], vf_refs[r][...].astype(BF16))
        acc = d if acc is None else acc + d
    accf_ref[...] = alpha_f * accf_ref[...] + acc

    @pl.when(g == n_steps - 1)
    def _():
        rr = lax.broadcasted_iota(jnp.int32, (rows, TPAD), 0) % TPAD
        ss = lax.broadcasted_iota(jnp.int32, (rows, TPAD), 1)
        causal = ss <= rr
        dist_n = (rr - ss).astype(F32)

        s = _dot_nt(qbd, kdn_ref[0].astype(BF16)) - slope[:, :TPAD] * dist_n
        alpha, p = online(md_ref, ld_ref, jnp.where(causal, s, NEG))
        vdn = vdn_ref[0].astype(BF16)
        pv = jnp.concatenate(
            [_dot(p[h * 2 * TPAD:(h + 1) * 2 * TPAD, :], vdn[:, h * LANES:(h + 1) * LANES])
             for h in range(H_DIFF)], axis=0)
        accd = alpha * accd_ref[...] + pv

        cn = cn_ref[0]
        cnt = cn.T
        bias = jnp.concatenate(
            [cn[:, h:h + 1] - cnt[h:h + 1, :] for h in range(H_FOX)], axis=0)
        s = _dot_nt(qbf, kfn_ref[0].astype(BF16)) + bias
        alpha, p = online(mf_ref, lf_ref, jnp.where(causal, s, NEG))
        accf = alpha * accf_ref[...] + _dot(p, vfn_ref[0].astype(BF16))

        lam = _lambda(lq1_ref[...], lk1_ref[...], lq2_ref[...], lk2_ref[...], lam_init)
        od = accd / ld_ref[...]
        for h in range(H_DIFF):
            od0 = od[(2 * h) * TPAD:(2 * h + 1) * TPAD]
            od1 = od[(2 * h + 1) * TPAD:(2 * h + 2) * TPAD]
            o = _rms(od0 - lam * od1, gs_ref[...]) * (1.0 - lam_init)
            o_ref[0, :, h * LANES:(h + 1) * LANES] = o
        of = accf / lf_ref[...]
        l512 = lax.broadcasted_iota(jnp.int32, (TPAD, D_FOX), 1)
        out_f = jnp.zeros((TPAD, D_FOX), F32)
        for h in range(H_FOX):
            out_f = jnp.where(l512 // HEAD_DIM == h, of[h * TPAD:(h + 1) * TPAD], out_f)
        o_ref[0, :, D_DIFF:] = out_f


def _sample_attn(page_table, qd, qf, kdn, vdn, kfn, vfn, cn, lam_vecs, g_sub,
                 kd_pages, vd_pages, kf_pages, vf_pages, lf_pages, *, layer, lam_init):
    batch, n_pages = page_table.shape
    assert n_pages % PP == 0
    n_steps = n_pages // PP
    rows = 2 * H_DIFF * TPAD

    def tok_spec(width):
        return pl.BlockSpec((1, TPAD, width), lambda b, g, pt: (b, 0, 0))

    vec64 = pl.BlockSpec((1, HEAD_DIM), lambda b, g, pt: (0, 0))
    hbm = pl.BlockSpec(memory_space=pl.ANY)
    in_specs = [tok_spec(512), tok_spec(512), tok_spec(512), tok_spec(512), tok_spec(512), tok_spec(512),
                tok_spec(LANES), vec64, vec64, vec64, vec64,
                pl.BlockSpec((1, DIFF_VDIM), lambda b, g, pt: (0, 0)), hbm, hbm, hbm, hbm, hbm]
    args = [qd, qf, kdn, vdn, kfn, vfn, cn, *lam_vecs, g_sub, kd_pages, vd_pages, kf_pages, vf_pages, lf_pages]
    page_buf = pltpu.VMEM((N_SLOT, PP, 512, LANES), F32)
    grid_spec = pltpu.PrefetchScalarGridSpec(
        num_scalar_prefetch=1,
        grid=(batch, n_steps),
        in_specs=in_specs,
        out_specs=pl.BlockSpec((1, TPAD, D_MIX), lambda b, g, pt: (b, 0, 0)),
        scratch_shapes=[pltpu.VMEM((rows, 512), BF16), pltpu.VMEM((rows, 512), BF16),
                        pltpu.VMEM((rows, 1), F32), pltpu.VMEM((rows, 1), F32), pltpu.VMEM((rows, DIFF_VDIM), F32),
                        pltpu.VMEM((rows, 1), F32), pltpu.VMEM((rows, 1), F32), pltpu.VMEM((rows, D_FOX), F32),
                        pltpu.VMEM((H_FOX, LANES), F32), pltpu.VMEM((rows, LANES), F32),
                        page_buf, page_buf, page_buf, page_buf, pltpu.VMEM((N_SLOT, PP, H_FOX, LANES), F32),
                        pltpu.SemaphoreType.DMA((N_SLOT, 5, PP))],
    )
    return pl.pallas_call(
        functools.partial(_sample_attn_body, layer=layer, batch=batch, n_pages=n_pages, lam_init=lam_init),
        grid_spec=grid_spec,
        out_shape=jax.ShapeDtypeStruct((batch, TPAD, D_MIX), F32),
        compiler_params=_cparams(("arbitrary", "arbitrary")),
        name="sample_attn",
    )(page_table.reshape(-1), *args)


def _prep_ffn(wg, wu, wd):
    return wg.astype(BF16), wu.astype(BF16), wd.astype(BF16)


def _tile_gain(g):
    return jnp.tile(g.astype(F32), 512 // HEAD_DIM).reshape(1, 512)


def _pad_t(a, t):
    b = a.shape[0] // t
    return jnp.pad(a.reshape(b, t, a.shape[1]), ((0, 0), (0, TPAD - t), (0, 0)))


def kernel(x_prompt, x_sample, cache_k_diff, cache_v_diff, cache_k_fox, cache_v_fox, cache_logf_fox, page_table, g_ffn1, w1_gate, w1_up, w1_down, g_attn, w_in, b_f, g_qd, g_kd, g_qf, g_kf, lam_q1, lam_k1, lam_q2, lam_k2, g_subln, w_out, g_ffn2, w2_gate, w2_up, w2_down):
    batch, seq, d = x_prompt.shape
    dbatch, dseq, _ = x_sample.shape
    depth = w_in.shape[0]
    n_pool = cache_k_diff.shape[1]

    kd_pages = jnp.transpose(cache_k_diff, (0, 1, 3, 4, 5, 2)).reshape(depth, n_pool, 512, PAGE)
    vd_pages = cache_v_diff.reshape(depth, n_pool, PAGE * H_DIFF, DIFF_VDIM)
    kf_pages = jnp.transpose(cache_k_fox, (0, 1, 3, 4, 2)).reshape(depth, n_pool, 512, PAGE)
    vf_pages = jnp.transpose(cache_v_fox, (0, 1, 3, 4, 2)).reshape(depth, n_pool, 512, PAGE)
    lf_pages = jnp.transpose(cache_logf_fox, (0, 1, 3, 2))

    xp = x_prompt.reshape(batch * seq, d)
    xs = x_sample.reshape(dbatch * dseq, d)
    outs_p = [[] for _ in range(5)]
    outs_s = [[] for _ in range(5)]
    for l in range(depth):
        lam_init = 0.8 - 0.6 * math.exp(-0.3 * l)
        ffn1 = _prep_ffn(w1_gate[l], w1_up[l], w1_down[l])
        ffn2 = _prep_ffn(w2_gate[l], w2_up[l], w2_down[l])
        win_main = w_in[l][:, :N_MAIN].astype(BF16)
        wf_pad = jnp.pad(w_in[l][:, N_MAIN:], ((0, 0), (0, LANES - H_FOX))).astype(BF16)
        bf_pad = jnp.pad(b_f[l].astype(F32), (0, LANES - H_FOX)).reshape(1, LANES)
        gains = (_tile_gain(g_qd[l]), _tile_gain(g_kd[l]), _tile_gain(g_qf[l]), _tile_gain(g_kf[l]))
        wo = w_out[l].astype(BF16)
        lam_vecs = tuple(v[l].astype(F32).reshape(1, HEAD_DIM) for v in (lam_q1, lam_k1, lam_q2, lam_k2))
        g_sub = g_subln[l].astype(F32).reshape(1, DIFF_VDIM)

        hp = _ffn(xp, g_ffn1[l], *ffn1)
        (qd, qf, kd, vd, kf, vf, lf, kdb, vdb, kfb, vfb, c) = _proj(
            hp, g_attn[l], win_main, wf_pad, bf_pad, *gains, seg=seq, q_dtype=BF16,
            q_scale=ATTN_SCALE * LOG2E, native=True)
        mix = _prompt_attn(qd, qf, kdb, vdb, kfb, vfb, c, lam_vecs, g_sub,
                           batch=batch, seq=seq, lam_init=lam_init)
        xp = _ffn(hp, g_ffn2[l], *ffn2, mix=mix, w_out=wo)
        for lst, a in zip(outs_p, (kd, vd, kf, vf, lf)):
            lst.append(a)

        hs = _ffn(xs, g_ffn1[l], *ffn1)
        (qd, qf, kd, vd, kf, vf, lf, _, _, _, _, c) = _proj(
            hs, g_attn[l], win_main, wf_pad, bf_pad, *gains, seg=dseq, q_dtype=F32,
            q_scale=ATTN_SCALE, native=False)
        mix = _sample_attn(page_table, _pad_t(qd, dseq), _pad_t(qf, dseq), _pad_t(kd, dseq), _pad_t(vd, dseq),
                           _pad_t(kf, dseq), _pad_t(vf, dseq), _pad_t(c, dseq), lam_vecs, g_sub,
                           kd_pages, vd_pages, kf_pages, vf_pages, lf_pages, layer=l, lam_init=lam_init)
        mix = mix[:, :dseq].reshape(dbatch * dseq, D_MIX)
        xs = _ffn(hs, g_ffn2[l], *ffn2, mix=mix, w_out=wo)
        for lst, a in zip(outs_s, (kd, vd, kf, vf, lf)):
            lst.append(a)

    def stack(lst, b, t, tail):
        return jnp.stack(lst).reshape(depth, b, t, *tail)

    tails = ((H_DIFF, 2, HEAD_DIM), (H_DIFF, DIFF_VDIM), (H_FOX, HEAD_DIM), (H_FOX, HEAD_DIM), (H_FOX,))
    kd_p, vd_p, kf_p, vf_p, lf_p = (jnp.stack(lst) for lst in outs_p)
    new_p = (
        kd_p.reshape(depth, batch, H_DIFF, 2, HEAD_DIM, seq).transpose(0, 1, 5, 2, 3, 4),
        vd_p.reshape(depth, batch, seq, H_DIFF, DIFF_VDIM),
        kf_p.reshape(depth, batch, H_FOX, HEAD_DIM, seq).transpose(0, 1, 4, 2, 3),
        vf_p.reshape(depth, batch, H_FOX, HEAD_DIM, seq).transpose(0, 1, 4, 2, 3),
        lf_p.transpose(0, 1, 3, 2),
    )
    return (xp.reshape(batch, seq, d), xs.reshape(dbatch, dseq, d), *new_p,
            *[stack(lst, dbatch, dseq, tail) for lst, tail in zip(outs_s, tails)])
```

```python
import functools
import math

import jax
import jax.numpy as jnp
from jax import lax
from jax.experimental import pallas as pl
from jax.experimental.pallas import tpu as pltpu

F32 = jnp.float32
BF16 = jnp.bfloat16

HEAD_DIM = 64
H_DIFF = 4
H_FOX = 8
DIFF_VDIM = 2 * HEAD_DIM
D_DIFF = H_DIFF * DIFF_VDIM
D_FOX = H_FOX * HEAD_DIM
D_MIX = D_DIFF + D_FOX
PAGE = 128
EPS = 1e-6
NEG = -1e30
ATTN_SCALE = HEAD_DIM ** -0.5
LOG2E = math.log2(math.e)
N_MAIN = 6 * 512

LANES = 128
SUBLANES = 8
VMEM_LIMIT = 56 * 1024 * 1024

TM = 512
TF = 256
TQ = 512
PP = 8
N_SLOT = 3
TPAD = 8


def _cparams(sem):
    return pltpu.CompilerParams(dimension_semantics=sem, vmem_limit_bytes=VMEM_LIMIT)


def _const_spec(shape):
    nd = len(shape)
    return pl.BlockSpec(shape, lambda *_: (0,) * nd, pipeline_mode=pl.Buffered(1))


def _rms(x, g):
    return x * lax.rsqrt(jnp.mean(x * x, axis=-1, keepdims=True) + EPS) * g


def _split3(x):
    p0 = x.astype(BF16)
    r1 = x - p0.astype(F32)
    p1 = r1.astype(BF16)
    p2 = (r1 - p1.astype(F32)).astype(BF16)
    return p0, p1, p2


def _dot(a, b):
    return jnp.dot(a, b, preferred_element_type=F32)


def _dot_nt(a, b):
    return lax.dot_general(a, b, (((1,), (1,)), ((), ())), preferred_element_type=F32)


def _exact_dot(t, x):
    p0, p1, p2 = _split3(x)
    return _dot(t, p0) + _dot(t, p1) + _dot(t, p2)


def _exact_dot_r(x, t):
    p0, p1, p2 = _split3(x)
    return _dot(p0, t) + _dot(p1, t) + _dot(p2, t)


def _ffn_body(*refs, tf, fuse_out):
    if fuse_out:
        x_ref, mix_ref, wo_ref, g_ref, wg_ref, wu_ref, wd_ref, o_ref, acc_ref, u_ref = refs
        x = x_ref[...] + _dot(mix_ref[...].astype(BF16), wo_ref[...])
    else:
        x_ref, g_ref, wg_ref, wu_ref, wd_ref, o_ref, acc_ref, u_ref = refs
        x = x_ref[...]
    o_ref[...] = x
    u_ref[...] = _rms(x, g_ref[...]).astype(BF16)
    n_chunks = wg_ref.shape[1] // tf

    def gate_up(c):
        u = u_ref[...]
        return _dot(u, wg_ref[:, c * tf:(c + 1) * tf]), _dot(u, wu_ref[:, c * tf:(c + 1) * tf])

    nxt = gate_up(0)
    for c in range(n_chunks):
        gate, up = nxt
        if c + 1 < n_chunks:
            nxt = gate_up(c + 1)
        a = (gate * jax.nn.sigmoid(gate) * up).astype(BF16)
        d = _dot(a, wd_ref[c * tf:(c + 1) * tf, :])
        if c == 0:
            acc_ref[...] = d
        else:
            acc_ref[...] += d
    o_ref[...] = o_ref[...] + 0.5 * acc_ref[...]


def _ffn(x, g, wg_c, wu_c, wd_c, mix=None, w_out=None):
    n, d = x.shape
    assert wg_c.shape[1] % TF == 0
    tm = min(TM, n)
    fuse = mix is not None
    tok = pl.BlockSpec((tm, d), lambda i: (i, 0))
    in_specs = [tok]
    args = [x]
    if fuse:
        in_specs += [pl.BlockSpec((tm, mix.shape[1]), lambda i: (i, 0)), _const_spec(w_out.shape)]
        args += [mix, w_out]
    in_specs += [_const_spec((1, d)), _const_spec(wg_c.shape), _const_spec(wu_c.shape), _const_spec(wd_c.shape)]
    args += [g.reshape(1, d), wg_c, wu_c, wd_c]
    return pl.pallas_call(
        functools.partial(_ffn_body, tf=TF, fuse_out=fuse),
        grid=(n // tm,),
        in_specs=in_specs,
        out_specs=tok,
        out_shape=jax.ShapeDtypeStruct((n, d), F32),
        scratch_shapes=[pltpu.VMEM((tm, d), F32), pltpu.VMEM((tm, d), BF16)],
        compiler_params=_cparams(("arbitrary",)),
        name="ffn_out" if fuse else "ffn",
    )(*args)


def _proj_body(h_ref, ga_ref, win_ref, wf_ref, bf_ref, gqd_ref, gkd_ref, gqf_ref, gkf_ref, *rest,
               tm, seg, q_scale, native, n_prev, all_layers):
    (qd_o, qf_o, kd_o, vd_o, kf_o, vf_o, lf_o, kdb_o, vdb_o, kfb_o, vfb_o, c_o, carry_ref) = rest[n_prev:]
    i = pl.program_id(0)
    u = _rms(h_ref[...], ga_ref[...]).astype(BF16)

    def put(o_ref, x):
        if all_layers:
            o_ref[0] = x
            o_ref[1:] = jnp.zeros((o_ref.shape[0] - 1,) + x.shape, x.dtype)
        else:
            o_ref[...] = x

    def put_rows(o_ref, x):
        put(o_ref, x.T if native else x)

    r = lax.broadcasted_iota(jnp.int32, (2 * LANES, 2 * LANES), 0)
    c = lax.broadcasted_iota(jnp.int32, (2 * LANES, 2 * LANES), 1)
    group = (r // HEAD_DIM == c // HEAD_DIM).astype(BF16)

    def head_norm(x, g):
        x2 = (x * x).astype(BF16)
        ss = jnp.concatenate([_dot(x2[:, :2 * LANES], group), _dot(x2[:, 2 * LANES:], group)], axis=1)
        return x * lax.rsqrt(ss * (1.0 / HEAD_DIM) + EPS) * g

    def seg_cols(k):
        return _dot(u, win_ref[:, k * 512:(k + 1) * 512])

    qd_o[...] = (head_norm(seg_cols(0), gqd_ref[...]) * q_scale).astype(qd_o.dtype)
    kd = head_norm(seg_cols(1), gkd_ref[...])
    put_rows(kd_o, kd)
    kdb_o[...] = kd.astype(BF16)
    vd = seg_cols(2)
    if native:
        vd_rows = vd_o.at[0] if all_layers else vd_o
        for h in range(H_DIFF):
            vd_rows[pl.ds(h, tm, stride=H_DIFF), :] = vd[:, h * DIFF_VDIM:(h + 1) * DIFF_VDIM]
        if all_layers:
            vd_o[1:] = jnp.zeros((vd_o.shape[0] - 1,) + vd_o.shape[1:], F32)
    else:
        vd_o[...] = vd
    vdb_o[...] = vd.astype(BF16)
    qf_o[...] = (head_norm(seg_cols(3), gqf_ref[...]) * q_scale).astype(qf_o.dtype)
    kf = head_norm(seg_cols(4), gkf_ref[...])
    put_rows(kf_o, kf)
    kfb_o[...] = kf.astype(BF16)
    vf = seg_cols(5)
    put_rows(vf_o, vf)
    vfb_o[...] = vf.astype(BF16)

    lane = lax.broadcasted_iota(jnp.int32, (tm, LANES), 1)
    logf = jnp.where(lane < H_FOX, jax.nn.log_sigmoid(_dot(u, wf_ref[...]) + bf_ref[...]), 0.0)
    put(lf_o, logf.T[:H_FOX, :] if native else logf[:, :H_FOX])

    ri = lax.broadcasted_iota(jnp.int32, (tm, tm), 0)
    ci = lax.broadcasted_iota(jnp.int32, (tm, tm), 1)
    if seg >= tm:
        tri = (ci <= ri).astype(BF16)
        tiles_per_seg = seg // tm

        @pl.when(i % tiles_per_seg == 0)
        def _():
            carry_ref[...] = jnp.zeros_like(carry_ref)

        csum = _exact_dot(tri, logf) + carry_ref[0:1, :]
        carry_ref[...] = jnp.broadcast_to(csum[tm - 1:tm, :], carry_ref.shape)
    else:
        tri = ((ci <= ri) & (ri // seg == ci // seg)).astype(BF16)
        csum = _exact_dot(tri, logf)
    c_o[...] = csum


def _proj(h, ga, win_main, wf_pad, bf_pad, gqd, gkd, gqf, gkf, *, seg, q_dtype, q_scale, native,
          layer=0, depth=1, prev=()):
    n, d = h.shape
    tm = min(TM, n)
    tok512 = pl.BlockSpec((tm, 512), lambda i: (i, 0))
    all_layers = native and layer == 0 and depth > 1
    if native:
        assert seg % tm == 0
        tps = seg // tm
        nseq = n // seg
        fm_shape = jax.ShapeDtypeStruct((depth, nseq, 512, seg), F32)
        lead = depth if all_layers else None
        fm_spec = pl.BlockSpec((lead, None, 512, tm), lambda i: (layer, i // tps, 0, i % tps))
        k_shapes = (fm_shape, jax.ShapeDtypeStruct((depth, n * H_DIFF, DIFF_VDIM), F32), fm_shape, fm_shape,
                    jax.ShapeDtypeStruct((depth, nseq, H_FOX, seg), F32))
        k_specs = (fm_spec, pl.BlockSpec((lead, tm * H_DIFF, DIFF_VDIM), lambda i: (layer, i, 0)), fm_spec, fm_spec,
                   pl.BlockSpec((lead, None, H_FOX, tm), lambda i: (layer, i // tps, 0, i % tps)))
    else:
        k_shapes = (jax.ShapeDtypeStruct((n, 512), F32),) * 4 + (jax.ShapeDtypeStruct((n, H_FOX), F32),)
        k_specs = (tok512,) * 4 + (pl.BlockSpec((tm, H_FOX), lambda i: (i, 0)),)
    out_shape = (
        jax.ShapeDtypeStruct((n, 512), q_dtype), jax.ShapeDtypeStruct((n, 512), q_dtype), *k_shapes,
        jax.ShapeDtypeStruct((n, 512), BF16), jax.ShapeDtypeStruct((n, 512), BF16),
        jax.ShapeDtypeStruct((n, 512), BF16), jax.ShapeDtypeStruct((n, 512), BF16),
        jax.ShapeDtypeStruct((n, LANES), F32),
    )
    out_specs = (tok512, tok512, *k_specs, tok512, tok512, tok512, tok512,
                 pl.BlockSpec((tm, LANES), lambda i: (i, 0)))
    n_in = 9
    return pl.pallas_call(
        functools.partial(_proj_body, tm=tm, seg=seg, q_scale=q_scale, native=native, n_prev=len(prev),
                          all_layers=all_layers),
        grid=(n // tm,),
        in_specs=[pl.BlockSpec((tm, d), lambda i: (i, 0)), _const_spec((1, d)),
                  _const_spec(win_main.shape), _const_spec(wf_pad.shape), _const_spec((1, LANES)),
                  _const_spec((1, 512)), _const_spec((1, 512)), _const_spec((1, 512)), _const_spec((1, 512))]
        + [pl.BlockSpec(memory_space=pl.ANY)] * len(prev),
        out_specs=out_specs,
        out_shape=out_shape,
        input_output_aliases={n_in + k: 2 + k for k in range(len(prev))},
        scratch_shapes=[pltpu.VMEM((SUBLANES, LANES), F32)],
        compiler_params=_cparams(("arbitrary",)),
        name="proj",
    )(h, ga.reshape(1, d), win_main, wf_pad, bf_pad, gqd, gkd, gqf, gkf, *prev)


def _lambda(lq1, lk1, lq2, lk2, lam_init):
    a = jnp.sum(lq1 * lk1, axis=-1, keepdims=True)
    b = jnp.sum(lq2 * lk2, axis=-1, keepdims=True)
    return jnp.exp(a) - jnp.exp(b) + lam_init


def _alibi_slope(h):
    return 2.0 ** (-8.0 * (h + 1) / H_DIFF)


X_FOX_K = 0
X_FOX_Q = 24
X_ALIBI_Q = 32
X_ALIBI_K = 35


def _prompt_attn_body(qd_ref, qf_ref, kd_ref, vd_ref, kf_ref, vf_ref, cq_ref, ck_ref,
                      lq1_ref, lk1_ref, lq2_ref, lk2_ref, gs_ref, o_ref,
                      q2_ref, m_ref, l_ref, acc_ref, *, tq, lam_init):
    i = pl.program_id(1)
    j = pl.program_id(2)
    n_chain = H_DIFF + H_FOX // 2
    lane = lax.broadcasted_iota(jnp.int32, (tq, LANES), 1)
    lo = lane < HEAD_DIM

    def group(ref, ch):
        g = ch % H_DIFF
        return ref[:, g * LANES:(g + 1) * LANES]

    def pieces(x):
        p0, p1, p2 = _split3(x)
        return p0.astype(F32), p1.astype(F32), p2.astype(F32)

    def place3(base, a, x):
        return jnp.where(lane == base, a[0], jnp.where(lane == base + 1, a[1], jnp.where(lane == base + 2, a[2], x)))

    @pl.when(j == 0)
    def _():
        r_local = lax.broadcasted_iota(jnp.int32, (tq, LANES), 0).astype(F32)
        for ch in range(n_chain):
            q = group(qd_ref if ch < H_DIFF else qf_ref, ch)
            q2_ref[ch, :tq, :LANES] = jnp.where(lo, q, jnp.zeros_like(q))
            q2_ref[ch, tq:, :LANES] = jnp.where(lo, jnp.zeros_like(q), q)
            if ch < H_DIFF:
                slope = _alibi_slope(ch)
                x = jnp.where((lane >= X_ALIBI_K) & (lane < X_ALIBI_K + 3), slope, 0.0)
                x = place3(X_ALIBI_Q, pieces(r_local * (-(slope * LOG2E))), x).astype(BF16)
                q2_ref[ch, :tq, LANES:] = x
                q2_ref[ch, tq:, LANES:] = x
            else:
                for e in range(2):
                    hh = 2 * (ch - H_DIFF) + e
                    pick = (lane == hh) | (lane == H_FOX + hh) | (lane == 2 * H_FOX + hh)
                    x = jnp.where(pick, 1.0, 0.0)
                    cq = jnp.broadcast_to(cq_ref[:, hh:hh + 1], (tq, LANES))
                    x = place3(X_FOX_Q, pieces(cq * LOG2E), x)
                    q2_ref[ch, e * tq:(e + 1) * tq, LANES:] = x.astype(BF16)
        m_ref[...] = jnp.full_like(m_ref, NEG)
        l_ref[...] = jnp.zeros_like(l_ref)
        acc_ref[...] = jnp.zeros_like(acc_ref)

    def key_extra():
        b = pieces(ck_ref[...] * (-LOG2E))
        x = jnp.where(lane < H_FOX, b[0], jnp.where(lane < 2 * H_FOX, pltpu.roll(b[1], H_FOX, 1),
                                                    pltpu.roll(b[2], 2 * H_FOX, 1)))
        ones = ((lane >= X_FOX_Q) & (lane < X_FOX_Q + 3)) | ((lane >= X_ALIBI_Q) & (lane < X_ALIBI_Q + 3))
        x = jnp.where(ones, 1.0, x)
        kpos = (lax.broadcasted_iota(jnp.int32, (tq, LANES), 0) + (j - i) * tq).astype(F32)
        return place3(X_ALIBI_K, pieces(kpos * LOG2E), x).astype(BF16)

    def step(masked):
        kx = key_extra()
        ones_cols = jnp.ones((tq, LANES), BF16)
        if masked:
            rr = lax.broadcasted_iota(jnp.int32, (tq, tq), 0)
            cc = lax.broadcasted_iota(jnp.int32, (tq, tq), 1)
            negmask = jnp.where(rr >= cc, 0.0, NEG)

        def scores(ch):
            kk = jnp.concatenate([group(kd_ref if ch < H_DIFF else kf_ref, ch), kx], axis=1)
            s = _dot_nt(q2_ref[ch], kk)
            if masked:
                s = jnp.concatenate([s[:tq] + negmask, s[tq:] + negmask], axis=0)
            return s

        def update(ch, s):
            v1 = jnp.concatenate([group(vd_ref if ch < H_DIFF else vf_ref, ch), ones_cols], axis=1)
            m_prev = m_ref[ch]
            m_new = jnp.maximum(m_prev, jnp.max(s, axis=-1, keepdims=True))
            alpha = jnp.exp2(m_prev - m_new)
            p = jnp.concatenate([jnp.exp2(s[:, c * LANES:(c + 1) * LANES] - m_new).astype(BF16)
                                 for c in range(tq // LANES)], axis=1)
            pv = _dot(p, v1)
            m_ref[ch] = m_new
            l_ref[ch] = alpha * l_ref[ch] + pv[:, LANES:]
            acc_ref[ch] = alpha * acc_ref[ch] + pv[:, :LANES]

        s_next = scores(0)
        for ch in range(n_chain):
            s_cur = s_next
            if ch + 1 < n_chain:
                s_next = scores(ch + 1)
            update(ch, s_cur)

    @pl.when(j < i)
    def _():
        step(False)

    @pl.when(j == i)
    def _():
        step(True)
        lam = _lambda(lq1_ref[...], lk1_ref[...], lq2_ref[...], lk2_ref[...], lam_init)
        for ch in range(n_chain):
            od = acc_ref[ch] / l_ref[ch]
            if ch < H_DIFF:
                o = _rms(od[:tq] - lam * od[tq:], gs_ref[...]) * (1.0 - lam_init)
                o_ref[:, ch * LANES:(ch + 1) * LANES] = o.astype(o_ref.dtype)
            else:
                g = ch - H_DIFF
                o = jnp.where(lo, od[:tq], od[tq:])
                o_ref[:, D_DIFF + g * LANES:D_DIFF + (g + 1) * LANES] = o.astype(o_ref.dtype)


def _prompt_attn(qd, qf, kdb, vdb, kfb, vfb, c, lam_vecs, g_sub, *, batch, seq, lam_init):
    tq = min(TQ, seq)
    nq = seq // tq
    n_chain = H_DIFF + H_FOX // 2
    q_spec = pl.BlockSpec((tq, 512), lambda b, i, j: (b * nq + i, 0))
    k_spec = pl.BlockSpec((tq, 512), lambda b, i, j: (b * nq + jnp.minimum(i, j), 0))
    vec64 = _const_spec((1, HEAD_DIM))
    return pl.pallas_call(
        functools.partial(_prompt_attn_body, tq=tq, lam_init=lam_init),
        grid=(batch, nq, nq),
        in_specs=[q_spec, q_spec, k_spec, k_spec, k_spec, k_spec,
                  pl.BlockSpec((tq, LANES), lambda b, i, j: (b * nq + i, 0)),
                  pl.BlockSpec((tq, LANES), lambda b, i, j: (b * nq + jnp.minimum(i, j), 0)),
                  vec64, vec64, vec64, vec64, _const_spec((1, DIFF_VDIM))],
        out_specs=pl.BlockSpec((tq, D_MIX), lambda b, i, j: (b * nq + i, 0)),
        out_shape=jax.ShapeDtypeStruct((batch * seq, D_MIX), BF16),
        scratch_shapes=[pltpu.VMEM((n_chain, 2 * tq, 2 * LANES), BF16),
                        pltpu.VMEM((n_chain, 2 * tq, LANES), F32), pltpu.VMEM((n_chain, 2 * tq, LANES), F32),
                        pltpu.VMEM((n_chain, 2 * tq, LANES), F32)],
        compiler_params=_cparams(("arbitrary", "arbitrary", "arbitrary")),
        name="prompt_attn",
    )(qd, qf, kdb, vdb, kfb, vfb, c, c, *lam_vecs, g_sub)


def _sample_attn_body(pt_ref, qd_ref, qf_ref, kdn_ref, vdn_ref, kfn_ref, vfn_ref, cn_ref,
                      lq1_ref, lk1_ref, lq2_ref, lk2_ref, gs_ref,
                      kd_hbm, vd_hbm, kf_hbm, vf_hbm, lfp_hbm, o_ref,
                      qbd_ref, qbf_ref, md_ref, ld_ref, accd_ref, mf_ref, lf_ref, accf_ref, sa_ref, cqb_ref,
                      kd_buf, vd_buf, kf_buf, vf_buf, lfp_buf, sem, *, layer, batch, n_pages, lam_init):
    b = pl.program_id(0)
    g = pl.program_id(1)
    n_steps = n_pages // PP
    rows = 2 * H_DIFF * TPAD

    step = b * n_steps + g
    n_total = batch * n_steps
    streams = ((kd_hbm, kd_buf), (vd_hbm, vd_buf), (kf_hbm, kf_buf), (vf_hbm, vf_buf), (lfp_hbm, lfp_buf))

    def page_copy(k, page, slot, r):
        hbm, buf = streams[k]
        return pltpu.make_async_copy(hbm.at[layer, page], buf.at[slot, r], sem.at[slot, k, r])

    def fetch(s):
        sb = s // n_steps
        base = sb * n_pages + n_pages - (s % n_steps + 1) * PP
        slot = s % N_SLOT
        for r in range(PP):
            page = pt_ref[base + r]
            for k in range(len(streams)):
                page_copy(k, page, slot, r).start()

    @pl.when(step == 0)
    def _():
        fetch(step)
        if n_total > 1:
            fetch(step + 1)

    @pl.when(step + 2 < n_total)
    def _():
        fetch(step + 2)

    cur = step % N_SLOT
    for r in range(PP):
        for k in range(len(streams)):
            page_copy(k, 0, cur, r).wait()
    kd_refs = [kd_buf.at[cur, r] for r in range(PP)]
    vd_refs = [vd_buf.at[cur, r] for r in range(PP)]
    kf_refs = [kf_buf.at[cur, r] for r in range(PP)]
    vf_refs = [vf_buf.at[cur, r] for r in range(PP)]
    lf_refs = [lfp_buf.at[cur, r] for r in range(PP)]

    row_i = lax.broadcasted_iota(jnp.int32, (rows, LANES), 0)
    lane_i = lax.broadcasted_iota(jnp.int32, (rows, LANES), 1)
    t_of_row = row_i % TPAD

    @pl.when(g == 0)
    def _():
        r512 = lax.broadcasted_iota(jnp.int32, (rows, 512), 0)
        l512 = lax.broadcasted_iota(jnp.int32, (rows, 512), 1)
        unit = r512 // TPAD == l512 // HEAD_DIM
        qd8 = jnp.concatenate([qd_ref[0]] * (rows // TPAD), axis=0)
        qf8 = jnp.concatenate([qf_ref[0]] * (rows // TPAD), axis=0)
        qbd_ref[...] = jnp.where(unit, qd8, 0.0).astype(BF16)
        qbf_ref[...] = jnp.where(unit, qf8, 0.0).astype(BF16)
        md_ref[...] = jnp.full_like(md_ref, NEG)
        mf_ref[...] = jnp.full_like(mf_ref, NEG)
        ld_ref[...] = jnp.zeros_like(ld_ref)
        lf_ref[...] = jnp.zeros_like(lf_ref)
        accd_ref[...] = jnp.zeros_like(accd_ref)
        accf_ref[...] = jnp.zeros_like(accf_ref)
        sa_ref[...] = jnp.zeros_like(sa_ref)
        cn = cn_ref[0]
        cqb_ref[...] = jnp.concatenate(
            [jnp.broadcast_to(cn[:, h:h + 1], (TPAD, LANES)) for h in range(H_FOX)], axis=0)

    slope = jnp.zeros((rows, LANES), F32)
    for h in range(H_DIFF):
        slope = jnp.where(row_i // (2 * TPAD) == h, _alibi_slope(h), slope)

    def online(m_ref, l_ref, s):
        m_prev = m_ref[...]
        m_new = jnp.maximum(m_prev, jnp.max(s, axis=-1, keepdims=True))
        alpha = jnp.exp(m_prev - m_new)
        p = jnp.exp(s - m_new)
        l_ref[...] = alpha * l_ref[...] + jnp.sum(p, axis=-1, keepdims=True)
        m_ref[...] = m_new
        return alpha, p.astype(BF16)

    p0 = n_pages - (g + 1) * PP
    ki = lax.broadcasted_iota(jnp.int32, (PAGE, PAGE), 0)
    kj = lax.broadcasted_iota(jnp.int32, (PAGE, PAGE), 1)
    later = (ki > kj).astype(BF16)
    qbd = qbd_ref[...]
    qbf = qbf_ref[...]
    past_len = n_pages * PAGE

    lf_all = jnp.concatenate([lf_refs[r][...] for r in range(PP)], axis=0)
    lf_pieces = _split3(lf_all)
    ones = jnp.ones((PAGE, PAGE), BF16)
    within = _dot(lf_pieces[0], later) + _dot(lf_pieces[1], later) + _dot(lf_pieces[2], later)
    totals = _dot(lf_pieces[0], ones) + _dot(lf_pieces[1], ones) + _dot(lf_pieces[2], ones)

    sd = [None] * PP
    sf = [None] * PP
    s_after = sa_ref[...]
    for r in reversed(range(PP)):
        rt = within[r * H_FOX:(r + 1) * H_FOX] + s_after
        s_after = s_after + totals[r * H_FOX:(r + 1) * H_FOX]
        rt_rows = jnp.concatenate(
            [jnp.broadcast_to(rt[h:h + 1, :], (TPAD, LANES)) for h in range(H_FOX)], axis=0)
        sf[r] = _dot(qbf, kf_refs[r][...].astype(BF16)) + (cqb_ref[...] + rt_rows)
        off = past_len - (p0 + r) * PAGE
        dist = (t_of_row - lane_i + off).astype(F32)
        sd[r] = _dot(qbd, kd_refs[r][...].astype(BF16)) - slope * dist
    sa_ref[...] = s_after

    alpha_d, p_d = online(md_ref, ld_ref, jnp.concatenate(sd, axis=1))
    alpha_f, p_f = online(mf_ref, lf_ref, jnp.concatenate(sf, axis=1))
    pv_heads = []
    for h in range(H_DIFF):
        acc = None
        for r in range(PP):
            v = vd_refs[r][pl.ds(h, PAGE, stride=H_DIFF), :].astype(BF16)
            d = _dot(p_d[h * 2 * TPAD:(h + 1) * 2 * TPAD, r * PAGE:(r + 1) * PAGE], v)
            acc = d if acc is None else acc + d
        pv_heads.append(acc)
    accd_ref[...] = alpha_d * accd_ref[...] + jnp.concatenate(pv_heads, axis=0)

    acc = None
    for r in range(PP):
        d = _dot_nt(p_f[:, r * PAGE:(r + 1) * PAGE], vf_refs[r][...].astype(BF16))
        acc = d if acc is None else acc + d
    accf_ref[...] = alpha_f * accf_ref[...] + acc

    @pl.when(g == n_steps - 1)
    def _():
        rr = lax.broadcasted_iota(jnp.int32, (rows, TPAD), 0) % TPAD
        ss = lax.broadcasted_iota(jnp.int32, (rows, TPAD), 1)
        causal = ss <= rr
        dist_n = (rr - ss).astype(F32)

        s = _dot_nt(qbd, kdn_ref[0].astype(BF16)) - slope[:, :TPAD] * dist_n
        alpha, p = online(md_ref, ld_ref, jnp.where(causal, s, NEG))
        vdn = vdn_ref[0].astype(BF16)
        pv = jnp.concatenate(
            [_dot(p[h * 2 * TPAD:(h + 1) * 2 * TPAD, :], vdn[:, h * LANES:(h + 1) * LANES])
             for h in range(H_DIFF)], axis=0)
        accd = alpha * accd_ref[...] + pv

        cn = cn_ref[0]
        cnt = cn.T
        bias = jnp.concatenate(
            [cn[:, h:h + 1] - cnt[h:h + 1, :] for h in range(H_FOX)], axis=0)
        s = _dot_nt(qbf, kfn_ref[0].astype(BF16)) + bias
        alpha, p = online(mf_ref, lf_ref, jnp.where(causal, s, NEG))
        accf = alpha * accf_ref[...] + _dot(p, vfn_ref[0].astype(BF16))

        lam = _lambda(lq1_ref[...], lk1_ref[...], lq2_ref[...], lk2_ref[...], lam_init)
        od = accd / ld_ref[...]
        for h in range(H_DIFF):
            od0 = od[(2 * h) * TPAD:(2 * h + 1) * TPAD]
            od1 = od[(2 * h + 1) * TPAD:(2 * h + 2) * TPAD]
            o = _rms(od0 - lam * od1, gs_ref[...]) * (1.0 - lam_init)
            o_ref[0, :, h * LANES:(h + 1) * LANES] = o
        of = accf / lf_ref[...]
        l512 = lax.broadcasted_iota(jnp.int32, (TPAD, D_FOX), 1)
        out_f = jnp.zeros((TPAD, D_FOX), F32)
        for h in range(H_FOX):
            out_f = jnp.where(l512 // HEAD_DIM == h, of[h * TPAD:(h + 1) * TPAD], out_f)
        o_ref[0, :, D_DIFF:] = out_f


def _sample_attn(page_table, qd, qf, kdn, vdn, kfn, vfn, cn, lam_vecs, g_sub,
                 kd_pages, vd_pages, kf_pages, vf_pages, lf_pages, *, layer, lam_init):
    batch, n_pages = page_table.shape
    assert n_pages % PP == 0
    n_steps = n_pages // PP
    rows = 2 * H_DIFF * TPAD

    def tok_spec(width):
        return pl.BlockSpec((1, TPAD, width), lambda b, g, pt: (b, 0, 0))

    vec64 = pl.BlockSpec((1, HEAD_DIM), lambda b, g, pt: (0, 0))
    hbm = pl.BlockSpec(memory_space=pl.ANY)
    in_specs = [tok_spec(512), tok_spec(512), tok_spec(512), tok_spec(512), tok_spec(512), tok_spec(512),
                tok_spec(LANES), vec64, vec64, vec64, vec64,
                pl.BlockSpec((1, DIFF_VDIM), lambda b, g, pt: (0, 0)), hbm, hbm, hbm, hbm, hbm]
    args = [qd, qf, kdn, vdn, kfn, vfn, cn, *lam_vecs, g_sub, kd_pages, vd_pages, kf_pages, vf_pages, lf_pages]
    page_buf = pltpu.VMEM((N_SLOT, PP, 512, LANES), F32)
    grid_spec = pltpu.PrefetchScalarGridSpec(
        num_scalar_prefetch=1,
        grid=(batch, n_steps),
        in_specs=in_specs,
        out_specs=pl.BlockSpec((1, TPAD, D_MIX), lambda b, g, pt: (b, 0, 0)),
        scratch_shapes=[pltpu.VMEM((rows, 512), BF16), pltpu.VMEM((rows, 512), BF16),
                        pltpu.VMEM((rows, 1), F32), pltpu.VMEM((rows, 1), F32), pltpu.VMEM((rows, DIFF_VDIM), F32),
                        pltpu.VMEM((rows, 1), F32), pltpu.VMEM((rows, 1), F32), pltpu.VMEM((rows, D_FOX), F32),
                        pltpu.VMEM((H_FOX, LANES), F32), pltpu.VMEM((rows, LANES), F32),
                        page_buf, page_buf, page_buf, page_buf, pltpu.VMEM((N_SLOT, PP, H_FOX, LANES), F32),
                        pltpu.SemaphoreType.DMA((N_SLOT, 5, PP))],
    )
    return pl.pallas_call(
        functools.partial(_sample_attn_body, layer=layer, batch=batch, n_pages=n_pages, lam_init=lam_init),
        grid_spec=grid_spec,
        out_shape=jax.ShapeDtypeStruct((batch, TPAD, D_MIX), F32),
        compiler_params=_cparams(("arbitrary", "arbitrary")),
        name="sample_attn",
    )(page_table.reshape(-1), *args)


def _prep_ffn(wg, wu, wd):
    return wg.astype(BF16), wu.astype(BF16), wd.astype(BF16)


def _tile_gain(g):
    return jnp.tile(g.astype(F32), 512 // HEAD_DIM).reshape(1, 512)


def _pad_t(a, t):
    b = a.shape[0] // t
    return jnp.pad(a.reshape(b, t, a.shape[1]), ((0, 0), (0, TPAD - t), (0, 0)))


def kernel(x_prompt, x_sample, cache_k_diff, cache_v_diff, cache_k_fox, cache_v_fox, cache_logf_fox, page_table, g_ffn1, w1_gate, w1_up, w1_down, g_attn, w_in, b_f, g_qd, g_kd, g_qf, g_kf, lam_q1, lam_k1, lam_q2, lam_k2, g_subln, w_out, g_ffn2, w2_gate, w2_up, w2_down):
    batch, seq, d = x_prompt.shape
    dbatch, dseq, _ = x_sample.shape
    depth = w_in.shape[0]
    n_pool = cache_k_diff.shape[1]

    kd_pages = jnp.transpose(cache_k_diff, (0, 1, 3, 4, 5, 2)).reshape(depth, n_pool, 512, PAGE)
    vd_pages = cache_v_diff.reshape(depth, n_pool, PAGE * H_DIFF, DIFF_VDIM)
    kf_pages = jnp.transpose(cache_k_fox, (0, 1, 3, 4, 2)).reshape(depth, n_pool, 512, PAGE)
    vf_pages = jnp.transpose(cache_v_fox, (0, 1, 3, 4, 2)).reshape(depth, n_pool, 512, PAGE)
    lf_pages = jnp.transpose(cache_logf_fox, (0, 1, 3, 2))

    xp = x_prompt.reshape(batch * seq, d)
    xs = x_sample.reshape(dbatch * dseq, d)
    new_rows_p = ()
    outs_s = [[] for _ in range(5)]
    for l in range(depth):
        lam_init = 0.8 - 0.6 * math.exp(-0.3 * l)
        ffn1 = _prep_ffn(w1_gate[l], w1_up[l], w1_down[l])
        ffn2 = _prep_ffn(w2_gate[l], w2_up[l], w2_down[l])
        win_main = w_in[l][:, :N_MAIN].astype(BF16)
        wf_pad = jnp.pad(w_in[l][:, N_MAIN:], ((0, 0), (0, LANES - H_FOX))).astype(BF16)
        bf_pad = jnp.pad(b_f[l].astype(F32), (0, LANES - H_FOX)).reshape(1, LANES)
        gains = (_tile_gain(g_qd[l]), _tile_gain(g_kd[l]), _tile_gain(g_qf[l]), _tile_gain(g_kf[l]))
        wo = w_out[l].astype(BF16)
        lam_vecs = tuple(v[l].astype(F32).reshape(1, HEAD_DIM) for v in (lam_q1, lam_k1, lam_q2, lam_k2))
        g_sub = g_subln[l].astype(F32).reshape(1, DIFF_VDIM)

        hp = _ffn(xp, g_ffn1[l], *ffn1)
        (qd, qf, *new_rows_p, kdb, vdb, kfb, vfb, c) = _proj(
            hp, g_attn[l], win_main, wf_pad, bf_pad, *gains, seg=seq, q_dtype=BF16,
            q_scale=ATTN_SCALE * LOG2E, native=True, layer=l, depth=depth, prev=new_rows_p)
        mix = _prompt_attn(qd, qf, kdb, vdb, kfb, vfb, c, lam_vecs, g_sub,
                           batch=batch, seq=seq, lam_init=lam_init)
        xp = _ffn(hp, g_ffn2[l], *ffn2, mix=mix, w_out=wo)

        hs = _ffn(xs, g_ffn1[l], *ffn1)
        (qd, qf, kd, vd, kf, vf, lf, _, _, _, _, c) = _proj(
            hs, g_attn[l], win_main, wf_pad, bf_pad, *gains, seg=dseq, q_dtype=F32,
            q_scale=ATTN_SCALE, native=False)
        mix = _sample_attn(page_table, _pad_t(qd, dseq), _pad_t(qf, dseq), _pad_t(kd, dseq), _pad_t(vd, dseq),
                           _pad_t(kf, dseq), _pad_t(vf, dseq), _pad_t(c, dseq), lam_vecs, g_sub,
                           kd_pages, vd_pages, kf_pages, vf_pages, lf_pages, layer=l, lam_init=lam_init)
        mix = mix[:, :dseq].reshape(dbatch * dseq, D_MIX)
        xs = _ffn(hs, g_ffn2[l], *ffn2, mix=mix, w_out=wo)
        for lst, a in zip(outs_s, (kd, vd, kf, vf, lf)):
            lst.append(a)

    def stack(lst, b, t, tail):
        return jnp.stack(lst).reshape(depth, b, t, *tail)

    tails = ((H_DIFF, 2, HEAD_DIM), (H_DIFF, DIFF_VDIM), (H_FOX, HEAD_DIM), (H_FOX, HEAD_DIM), (H_FOX,))
    kd_p, vd_p, kf_p, vf_p, lf_p = new_rows_p
    new_p = (
        kd_p.reshape(depth, batch, H_DIFF, 2, HEAD_DIM, seq).transpose(0, 1, 5, 2, 3, 4),
        vd_p.reshape(depth, batch, seq, H_DIFF, DIFF_VDIM),
        kf_p.reshape(depth, batch, H_FOX, HEAD_DIM, seq).transpose(0, 1, 4, 2, 3),
        vf_p.reshape(depth, batch, H_FOX, HEAD_DIM, seq).transpose(0, 1, 4, 2, 3),
        lf_p.transpose(0, 1, 3, 2),
    )
    return (xp.reshape(batch, seq, d), xs.reshape(dbatch, dseq, d), *new_p,
            *[stack(lst, dbatch, dseq, tail) for lst, tail in zip(outs_s, tails)])
```
